```python
import math
import jax
import jax.numpy as jnp
from jax import lax
import numpy as np

D_MODEL = 1024
BATCH = 8
SEQ = 4096
DEPTH = 4

N_EVEN = (DEPTH + 1) // 2
N_ODD = DEPTH // 2
RWKV_HEAD_DIM = 64
RWKV_WIDTH = D_MODEL // 2
RWKV_HEADS = RWKV_WIDTH // RWKV_HEAD_DIM
W_LORA = 64
A_LORA = 64
G_LORA = 128
LNX_EPS = 64e-5
A_PROJ = 3 * RWKV_WIDTH + W_LORA + A_LORA + G_LORA
RWKV_SPLITS = [RWKV_WIDTH, 2 * RWKV_WIDTH, 3 * RWKV_WIDTH, 3 * RWKV_WIDTH + W_LORA, 3 * RWKV_WIDTH + W_LORA + A_LORA]
CONV_WIDTH = D_MODEL - RWKV_WIDTH
CONV_KERNEL = 31
CONV_PAD = CONV_KERNEL // 2
CONV_GROUPS = 8
B_PROJ = 2 * CONV_WIDTH
AB_PROJ = A_PROJ + B_PROJ
ATTN_HEAD_DIM = 64
ATTN_HEADS = D_MODEL // ATTN_HEAD_DIM
ATTN_KV_HEADS = max(ATTN_HEADS // 8, 1)
ATTN_GROUP = ATTN_HEADS // ATTN_KV_HEADS
Q_WIDTH = ATTN_HEADS * ATTN_HEAD_DIM
KV_WIDTH = ATTN_KV_HEADS * ATTN_HEAD_DIM
WINDOW = 128
ATTN_BLOCK = 128
N_BUCKETS = 32
MAX_DISTANCE = 128
N_EXPERTS = 32
TOP_K = 4
EXPERT_FF = D_MODEL
SWIGLU_LIMIT = 7.0
SWIGLU_ALPHA = 1.702
EXPERT_BLOCK = 128
RMS_EPS = 1e-6
GN_EPS = 1e-5
NEG = -1e30

kernel_name = "hybrid_rwkv7_conformer_swa_moe_encoder"


def rms_norm(x, g):
    xf = x.astype(jnp.float32)
    y = xf * lax.rsqrt(jnp.mean(xf * xf, axis=-1, keepdims=True) + RMS_EPS)
    return (y * g).astype(x.dtype)


def group_norm(x, g, b, groups, eps):
    shp = x.shape
    xf = x.astype(jnp.float32).reshape(*shp[:-1], groups, shp[-1] // groups)
    mu = jnp.mean(xf, axis=-1, keepdims=True)
    xc = xf - mu
    var = jnp.mean(xc * xc, axis=-1, keepdims=True)
    y = (xc * lax.rsqrt(var + eps)).reshape(shp)
    return (y * g + b).astype(x.dtype)


def bidir_token_shift(p, mu_prev, mu_next):
    prev = jnp.pad(p, ((0, 0), (1, 0), (0, 0)))[:, :-1]
    nxt = jnp.pad(p, ((0, 0), (0, 1), (0, 0)))[:, 1:]
    return p + mu_prev * (prev - p) + mu_next * (nxt - p)


def wkv7_scan(r, w, k, v, a, b, reverse):
    bsz, _, h, n = r.shape
    xs = tuple(jnp.moveaxis(t, 1, 0) for t in (r, w, k, v, a, b))

    def step(state, inp):
        r_t, w_t, k_t, v_t, a_t, b_t = inp
        sa = jnp.einsum('bhvk,bhk->bhv', state, a_t)
        state = (state * w_t[:, :, None, :] + sa[..., None] * b_t[:, :, None, :]
                 + v_t[..., None] * k_t[:, :, None, :])
        return state, jnp.einsum('bhvk,bhk->bhv', state, r_t)

    s0 = jnp.zeros((bsz, h, n, n), jnp.float32)
    _, ys = lax.scan(step, s0, xs, reverse=reverse)
    return jnp.moveaxis(ys, 0, 1)


def rwkv7_time_mix(pa, w0, w2, a0, a2, k_k, k_a, r_k, lnx_g, lnx_b, g2):
    pa = pa.astype(jnp.float32)
    r, k, v, w_lo, a_lo, g_lo = jnp.split(pa, RWKV_SPLITS, axis=-1)
    bsz, s, _ = r.shape

    def heads(t):
        return t.reshape(bsz, s, RWKV_HEADS, RWKV_HEAD_DIM)

    rh, kh, vh = heads(r), heads(k), heads(v)
    kk = heads(k * k_k)
    kk = kk / jnp.maximum(jnp.sqrt(jnp.sum(kk * kk, axis=-1, keepdims=True)), 1e-12)
    k_a_h = k_a.reshape(RWKV_HEADS, RWKV_HEAD_DIM)
    tw = jnp.tanh(w_lo)
    wkv = jnp.zeros_like(rh)
    bonus = jnp.zeros_like(rh)
    for d in range(2):
        w_log = -jax.nn.softplus(-(w0[d] + tw @ w2[d])) - 0.5
        decay = heads(jnp.exp(-jnp.exp(w_log)))
        a = heads(jax.nn.sigmoid(a0[d] + a_lo @ a2[d]))
        k_d = kh * (1.0 + (a - 1.0) * k_a_h)
        wkv = wkv + wkv7_scan(rh, decay, k_d, vh, -kk, kk * a, reverse=(d == 1))
        bonus = bonus + jnp.sum(rh * k_d * r_k, axis=-1, keepdims=True) * vh
    y = group_norm(wkv.reshape(bsz, s, RWKV_WIDTH), lnx_g, lnx_b, RWKV_HEADS, LNX_EPS)
    y = y + bonus.reshape(bsz, s, RWKV_WIDTH)
    g = jax.nn.sigmoid(g_lo) @ g2
    return y * g


def conformer_conv(pb, conv_w, conv_b, gn_g, gn_b):
    u, gate = jnp.split(pb, 2, axis=-1)
    hcv = u * jax.nn.sigmoid(gate)
    kern = conv_w[:, None, :].astype(hcv.dtype)
    hcv = lax.conv_general_dilated(hcv, kern, window_strides=(1,), padding=((CONV_PAD, CONV_PAD),),
                                   dimension_numbers=('NWC', 'WIO', 'NWC'),
                                   feature_group_count=CONV_WIDTH) + conv_b
    hcv = group_norm(hcv, gn_g, gn_b, CONV_GROUPS, GN_EPS)
    return jax.nn.silu(hcv)


def t5_bucket(rel):
    n = -rel
    half = N_BUCKETS // 2
    ret = jnp.where(n < 0, half, 0)
    n = jnp.abs(n)
    max_exact = half // 2
    nf = jnp.maximum(n, 1).astype(jnp.float32)
    large = max_exact + (jnp.log(nf / max_exact) / math.log(MAX_DISTANCE / max_exact)
                         * (half - max_exact)).astype(jnp.int32)
    large = jnp.minimum(large, half - 1)
    return ret + jnp.where(n < max_exact, n, large)


def banded_window_attention(q, k, v, sink, rel_bias):
    bsz, s, kvh, grp, dh = q.shape
    nb = s // ATTN_BLOCK
    qb = jnp.moveaxis(q.reshape(bsz, nb, ATTN_BLOCK, kvh, grp, dh), 1, 0)
    pad = ((0, 0), (ATTN_BLOCK, ATTN_BLOCK), (0, 0), (0, 0))
    kp = jnp.pad(k, pad).reshape(bsz, nb + 2, ATTN_BLOCK, kvh, dh)
    vp = jnp.pad(v, pad).reshape(bsz, nb + 2, ATTN_BLOCK, kvh, dh)
    rel = jnp.arange(3 * ATTN_BLOCK)[None, :] - ATTN_BLOCK - jnp.arange(ATTN_BLOCK)[:, None]
    bias = jnp.transpose(rel_bias[t5_bucket(rel)].astype(jnp.float32), (2, 0, 1))
    bias = jnp.where(jnp.abs(rel) <= WINDOW, bias, NEG).reshape(kvh, grp, ATTN_BLOCK, 3 * ATTN_BLOCK)
    sink_b = sink.astype(jnp.float32).reshape(1, kvh, grp, 1, 1)
    scale = ATTN_HEAD_DIM ** -0.5

    def block(args):
        qblk, i = args
        kblk = lax.dynamic_slice_in_dim(kp, i, 3, axis=1).reshape(bsz, 3 * ATTN_BLOCK, kvh, dh)
        vblk = lax.dynamic_slice_in_dim(vp, i, 3, axis=1).reshape(bsz, 3 * ATTN_BLOCK, kvh, dh)
        sc = jnp.einsum('bqkgd,bjkd->bkgqj', qblk, kblk).astype(jnp.float32) * scale + bias
        key_pos = (i - 1) * ATTN_BLOCK + jnp.arange(3 * ATTN_BLOCK)
        valid = (key_pos >= 0) & (key_pos < s)
        sc = jnp.where(valid, sc, NEG)
        m = jnp.maximum(jnp.max(sc, axis=-1, keepdims=True), sink_b)
        p = jnp.exp(sc - m)
        denom = jnp.sum(p, axis=-1, keepdims=True) + jnp.exp(sink_b - m)
        o = jnp.einsum('bkgqj,bjkd->bkgqd', p, vblk.astype(jnp.float32)) / denom
        return jnp.transpose(o, (0, 3, 1, 2, 4))

    out = lax.map(block, (qb, jnp.arange(nb)))
    return jnp.moveaxis(out, 0, 1).reshape(bsz, s, kvh * grp * dh).astype(q.dtype)


def moe_ffn(h, router_w, router_b, w1, b1, w2, b2):
    bsz, s, d = h.shape
    t = bsz * s
    xt = h.reshape(t, d)
    logits = (xt @ router_w + router_b).astype(jnp.float32)
    top_val, top_idx = lax.top_k(logits, TOP_K)
    gates = jax.nn.softmax(top_val, axis=-1)
    n_assign = t * TOP_K
    e_flat = top_idx.reshape(n_assign)
    tok_flat = jnp.arange(n_assign) // TOP_K
    g_flat = gates.reshape(n_assign)
    order = jnp.argsort(e_flat, stable=True)
    e_sorted = e_flat[order]
    counts = jnp.bincount(e_flat, length=N_EXPERTS)
    group_start = jnp.cumsum(counts) - counts
    padded = ((counts + EXPERT_BLOCK - 1) // EXPERT_BLOCK) * EXPERT_BLOCK
    padded_end = jnp.cumsum(padded)
    padded_start = padded_end - padded
    dest = padded_start[e_sorted] + (jnp.arange(n_assign) - group_start[e_sorted])
    n_slots = n_assign + N_EXPERTS * EXPERT_BLOCK
    n_blocks = n_slots // EXPERT_BLOCK
    slot_tok = jnp.full((n_slots,), t, jnp.int32).at[dest].set(tok_flat[order].astype(jnp.int32))
    slot_gate = jnp.zeros((n_slots,), jnp.float32).at[dest].set(g_flat[order])
    blk_expert = jnp.minimum(jnp.searchsorted(padded_end, jnp.arange(n_blocks) * EXPERT_BLOCK, side='right'),
                             N_EXPERTS - 1)
    x_pad = jnp.concatenate([xt, jnp.zeros((1, d), xt.dtype)], axis=0)
    xs = x_pad[slot_tok].reshape(n_blocks, EXPERT_BLOCK, d)

    def expert_block(args):
        xb, e = args
        hcat = xb @ w1[e] + b1[e]
        x_glu = jnp.minimum(hcat[:, 0::2], SWIGLU_LIMIT)
        x_lin = jnp.clip(hcat[:, 1::2], -SWIGLU_LIMIT, SWIGLU_LIMIT)
        act = x_glu * jax.nn.sigmoid(SWIGLU_ALPHA * x_glu) * (x_lin + 1.0)
        return act @ w2[e] + b2[e]

    ys = lax.map(expert_block, (xs, blk_expert)).reshape(n_slots, d)
    ys = ys * slot_gate[:, None]
    out = jax.ops.segment_sum(ys, slot_tok, num_segments=t + 1)[:t]
    return out.reshape(bsz, s, d).astype(h.dtype)


def setup_inputs(seed: int = 0) -> dict:
    key = jax.random.key(seed)
    ks = iter(jax.random.split(key, 64))
    f32 = jnp.float32
    D = D_MODEL
    NE, NO = N_EVEN, N_ODD

    def nrm(shape, scale):
        return jax.random.normal(next(ks), shape, f32) * scale

    def uni(shape, lo, hi):
        return jax.random.uniform(next(ks), shape, f32, lo, hi)

    return {
        "x": nrm((BATCH, SEQ, D), 1.0),
        "c": nrm((BATCH, D), 1.0),
        "mod_w": nrm((DEPTH, D, 6 * D), 0.3 * D ** -0.5),
        "mod_b": nrm((DEPTH, 6 * D), 0.1),
        "norm_mix_pre": 1.0 + nrm((DEPTH, D), 0.05),
        "norm_mix_post": 1.0 + nrm((DEPTH, D), 0.05),
        "norm_ffn_pre": 1.0 + nrm((DEPTH, D), 0.05),
        "norm_ffn_post": 1.0 + nrm((DEPTH, D), 0.05),
        "ab_w_in": nrm((NE, D, AB_PROJ), D ** -0.5),
        "ab_mu_prev": uni((NE, A_PROJ), 0.0, 0.5),
        "ab_mu_next": uni((NE, A_PROJ), 0.0, 0.5),
        "rwkv_w0": uni((NE, 2, RWKV_WIDTH), -6.0, -1.0),
        "rwkv_w2": nrm((NE, 2, W_LORA, RWKV_WIDTH), 0.1 * W_LORA ** -0.5),
        "rwkv_a0": nrm((NE, 2, RWKV_WIDTH), 0.5),
        "rwkv_a2": nrm((NE, 2, A_LORA, RWKV_WIDTH), 0.1 * A_LORA ** -0.5),
        "rwkv_k_k": 0.85 + nrm((NE, RWKV_WIDTH), 0.05),
        "rwkv_k_a": 1.0 + nrm((NE, RWKV_WIDTH), 0.05),
        "rwkv_r_k": nrm((NE, RWKV_HEADS, RWKV_HEAD_DIM), 0.1),
        "rwkv_lnx_g": 1.0 + nrm((NE, RWKV_WIDTH), 0.05),
        "rwkv_lnx_b": nrm((NE, RWKV_WIDTH), 0.02),
        "rwkv_g2": nrm((NE, G_LORA, RWKV_WIDTH), G_LORA ** -0.5),
        "conv_w": nrm((NE, CONV_KERNEL, CONV_WIDTH), CONV_KERNEL ** -0.5),
        "conv_b": nrm((NE, CONV_WIDTH), 0.02),
        "conv_gn_g": 1.0 + nrm((NE, CONV_WIDTH), 0.05),
        "conv_gn_b": nrm((NE, CONV_WIDTH), 0.02),
        "ab_w_out": nrm((NE, RWKV_WIDTH + CONV_WIDTH, D), (RWKV_WIDTH + CONV_WIDTH) ** -0.5),
        "attn_w_qkv": nrm((NO, D, Q_WIDTH + 2 * KV_WIDTH), D ** -0.5),
        "attn_sink": nrm((NO, ATTN_HEADS), 0.5),
        "attn_w_o": nrm((NO, Q_WIDTH, D), Q_WIDTH ** -0.5),
        "rel_bias": nrm((N_BUCKETS, ATTN_HEADS), 0.5),
        "router_w": nrm((DEPTH, D, N_EXPERTS), D ** -0.5),
        "router_b": nrm((DEPTH, N_EXPERTS), 0.01),
        "exp_w1": nrm((DEPTH, N_EXPERTS, D, 2 * EXPERT_FF), D ** -0.5),
        "exp_b1": nrm((DEPTH, N_EXPERTS, 2 * EXPERT_FF), 0.01),
        "exp_w2": nrm((DEPTH, N_EXPERTS, EXPERT_FF, D), EXPERT_FF ** -0.5),
        "exp_b2": nrm((DEPTH, N_EXPERTS, D), 0.01),
    }


def reference(x, c, mod_w, mod_b, norm_mix_pre, norm_mix_post, norm_ffn_pre, norm_ffn_post,
              ab_w_in, ab_mu_prev, ab_mu_next, rwkv_w0, rwkv_w2, rwkv_a0, rwkv_a2, rwkv_k_k, rwkv_k_a,
              rwkv_r_k, rwkv_lnx_g, rwkv_lnx_b, rwkv_g2, conv_w, conv_b, conv_gn_g, conv_gn_b, ab_w_out,
              attn_w_qkv, attn_sink, attn_w_o, rel_bias,
              router_w, router_b, exp_w1, exp_b1, exp_w2, exp_b2):
    bsz, s, _ = x.shape
    c_act = jax.nn.silu(c)
    for layer in range(DEPTH):
        j = layer // 2
        mod = (c_act @ mod_w[layer] + mod_b[layer])[:, None, :]
        sh1, sc1, gt1, sh2, sc2, gt2 = jnp.split(mod, 6, axis=-1)
        h = rms_norm(x, norm_mix_pre[layer]) * (1.0 + sc1) + sh1
        if layer % 2 == 0:
            p = h @ ab_w_in[j]
            pa = bidir_token_shift(p[..., :A_PROJ], ab_mu_prev[j], ab_mu_next[j])
            ya = rwkv7_time_mix(pa, rwkv_w0[j], rwkv_w2[j], rwkv_a0[j], rwkv_a2[j], rwkv_k_k[j],
                                rwkv_k_a[j], rwkv_r_k[j], rwkv_lnx_g[j], rwkv_lnx_b[j], rwkv_g2[j])
            yb = conformer_conv(p[..., A_PROJ:], conv_w[j], conv_b[j], conv_gn_g[j], conv_gn_b[j])
            y = jnp.concatenate([ya.astype(h.dtype), yb.astype(h.dtype)], axis=-1) @ ab_w_out[j]
        else:
            qkv = h @ attn_w_qkv[j]
            q, k, v = jnp.split(qkv, [Q_WIDTH, Q_WIDTH + KV_WIDTH], axis=-1)
            q = q.reshape(bsz, s, ATTN_KV_HEADS, ATTN_GROUP, ATTN_HEAD_DIM)
            k = k.reshape(bsz, s, ATTN_KV_HEADS, ATTN_HEAD_DIM)
            v = v.reshape(bsz, s, ATTN_KV_HEADS, ATTN_HEAD_DIM)
            y = banded_window_attention(q, k, v, attn_sink[j], rel_bias) @ attn_w_o[j]
        x = x + gt1 * rms_norm(y, norm_mix_post[layer])
        h = rms_norm(x, norm_ffn_pre[layer]) * (1.0 + sc2) + sh2
        y = moe_ffn(h, router_w[layer], router_b[layer], exp_w1[layer], exp_b1[layer],
                    exp_w2[layer], exp_b2[layer])
        x = x + gt2 * rms_norm(y, norm_ffn_post[layer])
    return x
```

```python
import functools

import jax
import jax.numpy as jnp
from jax import lax
from jax.experimental import pallas as pl
from jax.experimental.pallas import tpu as pltpu

F32 = jnp.float32
BF16 = jnp.bfloat16
I32 = jnp.int32
HIGHEST = lax.Precision.HIGHEST

D_MODEL = 1024
HEAD_DIM = 64
RWKV_WIDTH = 512
RWKV_HEADS = RWKV_WIDTH // HEAD_DIM
W_LORA = 64
A_LORA = 64
G_LORA = 128
A_PROJ = 3 * RWKV_WIDTH + W_LORA + A_LORA + G_LORA
CONV_WIDTH = D_MODEL - RWKV_WIDTH
CONV_KERNEL = 31
CONV_PAD = CONV_KERNEL // 2
CONV_GROUPS = 8
ATTN_HEADS = 16
ATTN_KV_HEADS = 2
ATTN_GROUP = ATTN_HEADS // ATTN_KV_HEADS
Q_WIDTH = ATTN_HEADS * HEAD_DIM
KV_WIDTH = ATTN_KV_HEADS * HEAD_DIM
ATTN_BLOCK = 128
WINDOW = 128
N_BUCKETS = 32
MAX_DISTANCE = 128
N_EXPERTS = 32
TOP_K = 4
SWIGLU_LIMIT = 7.0
SWIGLU_ALPHA = 1.702
RMS_EPS = 1e-6
GN_EPS = 1e-5
LNX_EPS = 64e-5
NEG = -1e30

CHUNK = 64
HALO = 16
MOE_BLOCK = 256
VMEM_LIMIT = 56 * 1024 * 1024


def _params(sem, vmem=None):
    kw = dict(dimension_semantics=sem)
    if vmem is not None:
        kw["vmem_limit_bytes"] = vmem
    return pltpu.CompilerParams(**kw)


def _sigmoid(x):
    return 1.0 / (1.0 + jnp.exp(-x))


def _dot(a, b):
    return jnp.dot(a, b, preferred_element_type=F32)


def _dot_nt(a, b):
    return lax.dot_general(a, b, (((1,), (1,)), ((), ())), preferred_element_type=F32)


def _dot_exact_lhs(a_bf16, x):
    xh = x.astype(BF16)
    xl = (x - xh.astype(F32)).astype(BF16)
    return _dot(a_bf16, xh) + _dot(a_bf16, xl)


def _dot_exact_rhs(x, a_bf16):
    xh = x.astype(BF16)
    xl = (x - xh.astype(F32)).astype(BF16)
    return _dot(xh, a_bf16) + _dot(xl, a_bf16)


def _group_matrix(width, group, value):
    r = lax.broadcasted_iota(I32, (width, width), 0) // group
    c = lax.broadcasted_iota(I32, (width, width), 1) // group
    return jnp.where(r == c, value, 0.0).astype(BF16)


def _rms_norm(x, g):
    ms = jnp.mean(x * x, axis=-1, keepdims=True)
    return x * lax.rsqrt(ms + RMS_EPS) * g


def _norm_mod(x, g, sc, sh):
    return _rms_norm(x, g) * (1.0 + sc) + sh


def _mod_kernel(c_ref, w_ref, b_ref, o_ref):
    c = c_ref[...]
    ca = c * _sigmoid(c)
    o_ref[0] = jnp.dot(ca, w_ref[0], preferred_element_type=F32, precision=HIGHEST) + b_ref[0]


def _mod_all(c, mod_w, mod_b):
    depth, d, n = mod_w.shape
    bsz = c.shape[0]
    tn = 1536
    return pl.pallas_call(
        _mod_kernel,
        grid=(depth, n // tn),
        in_specs=[pl.BlockSpec((bsz, d), lambda l, j: (0, 0)),
                  pl.BlockSpec((1, d, tn), lambda l, j: (l, 0, j)),
                  pl.BlockSpec((1, 1, tn), lambda l, j: (l, 0, j))],
        out_specs=pl.BlockSpec((1, bsz, tn), lambda l, j: (l, 0, j)),
        out_shape=jax.ShapeDtypeStruct((depth, bsz, n), F32),
        compiler_params=_params(("parallel", "parallel"), VMEM_LIMIT),
        name="mod",
    )(c, mod_w, mod_b.reshape(depth, 1, n))


def _in_proj_kernel(x_ref, g_ref, sc_ref, sh_ref, w_ref, *o_refs):
    h = _norm_mod(x_ref[...], g_ref[...], sc_ref[0], sh_ref[0])
    p = _dot(h.astype(BF16), w_ref[...])
    off = 0
    for o_ref in o_refs:
        wdt = o_ref.shape[1]
        o_ref[...] = p[:, off:off + wdt]
        off += wdt


def _in_proj(x2, g, sc, sh, w_bf16, splits, seq, tm=256):
    t, d = x2.shape
    n = w_bf16.shape[1]
    tpb = seq // tm
    row = lambda i: (i, 0)
    per_b = lambda i: (i // tpb, 0, 0)
    return pl.pallas_call(
        _in_proj_kernel,
        grid=(t // tm,),
        in_specs=[pl.BlockSpec((tm, d), row),
                  pl.BlockSpec((1, d), lambda i: (0, 0)),
                  pl.BlockSpec((1, 1, d), per_b),
                  pl.BlockSpec((1, 1, d), per_b),
                  pl.BlockSpec((d, n), lambda i: (0, 0))],
        out_specs=[pl.BlockSpec((tm, s), row) for s in splits],
        out_shape=[jax.ShapeDtypeStruct((t, s), F32) for s in splits],
        compiler_params=_params(("parallel",), VMEM_LIMIT),
        name="in_proj",
    )(x2, g.reshape(1, d), sc, sh, w_bf16)


def _rwkv_prep_kernel(pm_ref, pp_ref, pn_ref, mup_ref, mun_ref, w0_ref, w2_ref, a0_ref, a2_ref,
                      kk_ref, ka_ref, rk_ref, g2_ref,
                      r_o, v_o, kk_o, kd_o, ad_o, lw_o, bonus_o, g_o):
    i = pl.program_id(1)
    n = pl.num_programs(1)
    p = pm_ref[...]
    ts = p.shape[0]
    rows = lax.broadcasted_iota(I32, (ts, 1), 0)
    prev_row = jnp.where(i > 0, pp_ref[HALO - 1:HALO, :], 0.0)
    next_row = jnp.where(i < n - 1, pn_ref[0:1, :], 0.0)
    prev = jnp.where(rows == 0, prev_row, pltpu.roll(p, 1, 0))
    nxt = jnp.where(rows == ts - 1, next_row, pltpu.roll(p, ts - 1, 0))
    pa = p + mup_ref[...] * (prev - p) + mun_ref[...] * (nxt - p)

    w = RWKV_WIDTH
    r = pa[:, 0:w]
    k = pa[:, w:2 * w]
    v = pa[:, 2 * w:3 * w]
    w_lo = pa[:, 3 * w:3 * w + W_LORA]
    a_lo = pa[:, 3 * w + W_LORA:3 * w + W_LORA + A_LORA]
    g_lo = pa[:, 3 * w + W_LORA + A_LORA:]

    ones = _group_matrix(w, HEAD_DIM, 1.0)
    kk = k * kk_ref[...]
    ss = _dot_exact_rhs(kk * kk, ones)
    kk = kk / jnp.maximum(jnp.sqrt(ss), 1e-12)
    tw = jnp.tanh(w_lo)
    r_o[...] = r
    v_o[...] = v
    kk_o[...] = kk
    bonus = jnp.zeros_like(r)
    for d in range(2):
        z = w0_ref[d] + jnp.dot(tw, w2_ref[d], preferred_element_type=F32, precision=HIGHEST)
        nz = -z
        softplus = jnp.maximum(nz, 0.0) + jnp.log(1.0 + jnp.exp(-jnp.abs(nz)))
        w_log = -softplus - 0.5
        lw_o[d] = -jnp.exp(w_log)
        a = _sigmoid(a0_ref[d] + jnp.dot(a_lo, a2_ref[d], preferred_element_type=F32,
                                         precision=HIGHEST))
        k_d = k * (1.0 + (a - 1.0) * ka_ref[...])
        ad_o[d] = a
        kd_o[d] = k_d
        bonus = bonus + _dot_exact_rhs(r * k_d * rk_ref[...], ones) * v
    bonus_o[...] = bonus
    g_o[...] = jnp.dot(_sigmoid(g_lo), g2_ref[...], preferred_element_type=F32, precision=HIGHEST)


def _rwkv_prep(pa, bsz, seq, mu_prev, mu_next, w0, w2, a0, a2, k_k, k_a, r_k, g2, ts=256):
    t = pa.shape[0]
    w = RWKV_WIDTH
    nt = seq // ts
    hb = ts // HALO
    main = lambda b, i: (b * nt + i, 0)
    prev = lambda b, i: (jnp.maximum((b * nt + i) * hb - 1, 0), 0)
    nxt = lambda b, i: (jnp.minimum((b * nt + i + 1) * hb, t // HALO - 1), 0)
    c0 = lambda b, i: (0, 0)
    c3 = lambda b, i: (0, 0, 0)
    dmain = lambda b, i: (0, b * nt + i, 0)
    tok = jax.ShapeDtypeStruct((t, w), F32)
    tok2 = jax.ShapeDtypeStruct((2, t, w), F32)
    return pl.pallas_call(
        _rwkv_prep_kernel,
        grid=(bsz, nt),
        in_specs=[pl.BlockSpec((ts, A_PROJ), main),
                  pl.BlockSpec((HALO, A_PROJ), prev),
                  pl.BlockSpec((HALO, A_PROJ), nxt),
                  pl.BlockSpec((1, A_PROJ), c0),
                  pl.BlockSpec((1, A_PROJ), c0),
                  pl.BlockSpec((2, 1, w), c3),
                  pl.BlockSpec((2, W_LORA, w), c3),
                  pl.BlockSpec((2, 1, w), c3),
                  pl.BlockSpec((2, A_LORA, w), c3),
                  pl.BlockSpec((1, w), c0),
                  pl.BlockSpec((1, w), c0),
                  pl.BlockSpec((1, w), c0),
                  pl.BlockSpec((G_LORA, w), c0)],
        out_specs=[pl.BlockSpec((ts, w), main)] * 3
                  + [pl.BlockSpec((2, ts, w), dmain)] * 3
                  + [pl.BlockSpec((ts, w), main)] * 2,
        out_shape=[tok, tok, tok, tok2, tok2, tok2, tok, tok],
        compiler_params=_params(("parallel", "parallel"), VMEM_LIMIT),
        name="rwkv_prep",
    )(pa, pa, pa, mu_prev.reshape(1, -1), mu_next.reshape(1, -1), w0.reshape(2, 1, w), w2,
      a0.reshape(2, 1, w), a2, k_k.reshape(1, w), k_a.reshape(1, w), r_k.reshape(1, w), g2)


def _wkv_chunk(r, kd, v, kk, a, lw, st, reverse):
    c = r.shape[0]
    hd = HEAD_DIM
    ri = lax.broadcasted_iota(I32, (c, c), 0)
    ci = lax.broadcasted_iota(I32, (c, c), 1)
    if reverse:
        incl, strict, last = ci >= ri, ci > ri, 0
    else:
        incl, strict, last = ci <= ri, ci < ri, c - 1
    cum = _dot_exact_lhs(incl.astype(BF16), lw)
    cum_last = cum[last:last + 1, :]
    w_in = jnp.exp(cum)
    w_inv = jnp.exp(-cum)
    w_ex = jnp.exp(cum - lw)
    w_end = jnp.exp(cum_last - cum)
    w_c = jnp.exp(cum_last)
    b = kk * a
    rt = r * w_in
    kt = kd * w_inv
    bt = b * w_inv
    at = -kk * w_ex
    bk_t = jnp.concatenate([b * w_end, kd * w_end], axis=0).T
    eye = (ri == ci).astype(F32)
    zeros = jnp.zeros((c, hd), F32)
    ys, sts = [], []
    for h in range(r.shape[1] // hd):
        sl = slice(h * hd, (h + 1) * hd)
        ll = _dot_nt(jnp.concatenate([at[:, sl], rt[:, sl]], axis=0),
                     jnp.concatenate([bt[:, sl], kt[:, sl]], axis=0))
        l_ab = jnp.where(strict, ll[:c, :c], 0.0)
        l_ak = jnp.where(strict, ll[:c, c:], 0.0)
        l_rb = jnp.where(incl, ll[c:, :c], 0.0)
        l_rk = jnp.where(incl, ll[c:, c:], 0.0)
        tinv = eye + l_ab
        lp = l_ab
        n_sq = max(1, (c - 1).bit_length() - 1)
        for _ in range(n_sq):
            lp = _dot(lp, lp)
            tinv = tinv + _dot(lp, tinv)
        vh = v[:, sl]
        x = _dot(tinv, jnp.concatenate([at[:, sl], _dot(l_ak, vh)], axis=1))
        rhs2 = jnp.concatenate([x, jnp.concatenate([zeros, vh], axis=1)], axis=0)
        lhs2 = jnp.concatenate([jnp.concatenate([l_rb, l_rk], axis=1), bk_t[sl, :]], axis=0)
        o2 = _dot(lhs2, rhs2)
        r_hat = rt[:, sl] + o2[:c, :hd]
        y_hat = o2[:c, hd:]
        m_t = eye[:hd, :hd] * w_c[:, sl] + o2[c:, :hd]
        n_t = o2[c:, hd:]
        o3 = _dot(jnp.concatenate([r_hat, m_t], axis=0), st[:, sl])
        ys.append(o3[:c] + y_hat)
        sts.append(o3[c:] + n_t)
    return jnp.concatenate(ys, axis=1), jnp.concatenate(sts, axis=1)


def _rwkv_scan_kernel(rf, vf, kkf, kdf, adf, lwf, rb, vb, kkb, kdb, adb, lwb, yf_o, yb_o, st_ref):
    @pl.when(pl.program_id(1) == 0)
    def _():
        st_ref[...] = jnp.zeros_like(st_ref)

    y, s = _wkv_chunk(rf[...], kdf[0], vf[...], kkf[...], adf[0], lwf[0], st_ref[0], False)
    yf_o[...] = y
    st_ref[0] = s
    y, s = _wkv_chunk(rb[...], kdb[0], vb[...], kkb[...], adb[0], lwb[0], st_ref[1], True)
    yb_o[...] = y
    st_ref[1] = s


def _rwkv_scan(r, v, kk, kd, ad, lw, bsz, seq):
    t, w = r.shape
    nc = seq // CHUNK
    fwd = lambda b, i: (b * nc + i, 0)
    bwd = lambda b, i: (b * nc + nc - 1 - i, 0)
    fwd0 = lambda b, i: (0, b * nc + i, 0)
    bwd1 = lambda b, i: (1, b * nc + nc - 1 - i, 0)
    tok = pl.BlockSpec((CHUNK, w), fwd)
    tokb = pl.BlockSpec((CHUNK, w), bwd)
    dir0 = pl.BlockSpec((1, CHUNK, w), fwd0)
    dir1 = pl.BlockSpec((1, CHUNK, w), bwd1)
    return pl.pallas_call(
        _rwkv_scan_kernel,
        grid=(bsz, nc),
        in_specs=[tok, tok, tok, dir0, dir0, dir0, tokb, tokb, tokb, dir1, dir1, dir1],
        out_specs=[tok, tokb],
        out_shape=[jax.ShapeDtypeStruct((t, w), F32)] * 2,
        scratch_shapes=[pltpu.VMEM((2, HEAD_DIM, w), F32)],
        compiler_params=_params(("parallel", "arbitrary"), VMEM_LIMIT),
        name="rwkv_scan",
    )(r, v, kk, kd, ad, lw, r, v, kk, kd, ad, lw)


def _conv_kernel(pm_ref, pp_ref, pn_ref, cw_ref, cb_ref, g_ref, b_ref, o_ref, hbuf):
    i = pl.program_id(1)
    n = pl.num_programs(1)
    ts = pm_ref.shape[0]
    cwid = CONV_WIDTH

    def glu(p):
        return p[:, :cwid] * _sigmoid(p[:, cwid:])

    hbuf[0:HALO, :] = jnp.where(i > 0, glu(pp_ref[...]), 0.0)
    hbuf[HALO:HALO + ts, :] = glu(pm_ref[...])
    hbuf[HALO + ts:2 * HALO + ts, :] = jnp.where(i < n - 1, glu(pn_ref[...]), 0.0)
    acc = jnp.zeros((ts, cwid), F32)
    for k in range(CONV_KERNEL):
        off = HALO - CONV_PAD + k
        acc = acc + cw_ref[k:k + 1, :] * hbuf[off:off + ts, :]
    acc = acc + cb_ref[...]
    avg = _group_matrix(cwid, cwid // CONV_GROUPS, 1.0 / (cwid // CONV_GROUPS))
    mu = _dot_exact_rhs(acc, avg)
    xc = acc - mu
    var = _dot_exact_rhs(xc * xc, avg)
    y = xc * lax.rsqrt(var + GN_EPS) * g_ref[...] + b_ref[...]
    o_ref[...] = y * _sigmoid(y)


def _conformer_conv(pb, bsz, seq, conv_w, conv_b, gn_g, gn_b, ts=512):
    t = pb.shape[0]
    cwid = CONV_WIDTH
    nt = seq // ts
    hb = ts // HALO
    main = lambda b, i: (b * nt + i, 0)
    prev = lambda b, i: (jnp.maximum((b * nt + i) * hb - 1, 0), 0)
    nxt = lambda b, i: (jnp.minimum((b * nt + i + 1) * hb, t // HALO - 1), 0)
    c0 = lambda b, i: (0, 0)
    return pl.pallas_call(
        _conv_kernel,
        grid=(bsz, nt),
        in_specs=[pl.BlockSpec((ts, 2 * cwid), main),
                  pl.BlockSpec((HALO, 2 * cwid), prev),
                  pl.BlockSpec((HALO, 2 * cwid), nxt),
                  pl.BlockSpec((CONV_KERNEL, cwid), c0),
                  pl.BlockSpec((1, cwid), c0),
                  pl.BlockSpec((1, cwid), c0),
                  pl.BlockSpec((1, cwid), c0)],
        out_specs=pl.BlockSpec((ts, cwid), main),
        out_shape=jax.ShapeDtypeStruct((t, cwid), F32),
        scratch_shapes=[pltpu.VMEM((ts + 2 * HALO, cwid), F32)],
        compiler_params=_params(("parallel", "parallel"), VMEM_LIMIT),
        name="conformer_conv",
    )(pb, pb, pb, conv_w, conv_b.reshape(1, cwid), gn_g.reshape(1, cwid), gn_b.reshape(1, cwid))


def _mix_out_kernel(yf_ref, yb_ref, bonus_ref, g_ref, cv_ref, x_ref, lg_ref, lb_ref, w_ref,
                    ng_ref, gt_ref, o_ref):
    w = RWKV_WIDTH
    wkv = yf_ref[...] + yb_ref[...]
    avg = _group_matrix(w, HEAD_DIM, 1.0 / HEAD_DIM)
    mu = _dot_exact_rhs(wkv, avg)
    xc = wkv - mu
    var = _dot_exact_rhs(xc * xc, avg)
    y = xc * lax.rsqrt(var + LNX_EPS) * lg_ref[...] + lb_ref[...] + bonus_ref[...]
    ya = y * g_ref[...]
    cat = jnp.concatenate([ya.astype(BF16), cv_ref[...].astype(BF16)], axis=1)
    z = _dot(cat, w_ref[...])
    o_ref[...] = x_ref[...] + gt_ref[0] * _rms_norm(z, ng_ref[...])


def _mix_out(yf, yb, bonus, g, cv, x2, lnx_g, lnx_b, w_out_bf16, norm_g, gt, seq, tm=256):
    t, d = x2.shape
    w = RWKV_WIDTH
    tpb = seq // tm
    row = lambda i: (i, 0)
    c0 = lambda i: (0, 0)
    half = pl.BlockSpec((tm, w), row)
    return pl.pallas_call(
        _mix_out_kernel,
        grid=(t // tm,),
        in_specs=[half, half, half, half, half,
                  pl.BlockSpec((tm, d), row),
                  pl.BlockSpec((1, w), c0),
                  pl.BlockSpec((1, w), c0),
                  pl.BlockSpec((d, d), c0),
                  pl.BlockSpec((1, d), c0),
                  pl.BlockSpec((1, 1, d), lambda i: (i // tpb, 0, 0))],
        out_specs=pl.BlockSpec((tm, d), row),
        out_shape=jax.ShapeDtypeStruct((t, d), F32),
        compiler_params=_params(("parallel",), VMEM_LIMIT),
        name="mix_out",
    )(yf, yb, bonus, g, cv, x2, lnx_g.reshape(1, w), lnx_b.reshape(1, w), w_out_bf16,
      norm_g.reshape(1, d), gt)


def _attn_kernel(sink_ref, q_ref, kp_ref, kc_ref, kn_ref, vp_ref, vc_ref, vn_ref, bias_ref, o_ref):
    i = pl.program_id(1)
    n = pl.num_programs(1)
    blk = ATTN_BLOCK
    hd = HEAD_DIM
    k = jnp.concatenate([kp_ref[...], kc_ref[...], kn_ref[...]], axis=0).astype(BF16)
    v = jnp.concatenate([vp_ref[...], vc_ref[...], vn_ref[...]], axis=0).astype(BF16)
    col = lax.broadcasted_iota(I32, (1, 3 * blk), 1)
    valid = ((col >= blk) | (i > 0)) & ((col < 2 * blk) | (i < n - 1))
    scale = hd ** -0.5
    for kvh in range(ATTN_KV_HEADS):
        kh = k[:, kvh * hd:(kvh + 1) * hd]
        vh = v[:, kvh * hd:(kvh + 1) * hd]
        for g in range(ATTN_GROUP):
            h = kvh * ATTN_GROUP + g
            qh = q_ref[:, h * hd:(h + 1) * hd].astype(BF16)
            sc = _dot_nt(qh, kh) * scale + bias_ref[h]
            sc = jnp.where(valid, sc, NEG)
            sink = sink_ref[h]
            m = jnp.maximum(jnp.max(sc, axis=-1, keepdims=True), sink)
            p = jnp.exp(sc - m)
            denom = jnp.sum(p, axis=-1, keepdims=True) + jnp.exp(sink - m)
            o_ref[:, h * hd:(h + 1) * hd] = _dot(p.astype(BF16), vh) / denom


def _attention(qkv, bias, sink, bsz, seq):
    t = qkv.shape[0]
    blk = ATTN_BLOCK
    nb = seq // blk
    kcol = Q_WIDTH // KV_WIDTH
    vcol = kcol + 1
    cur = lambda b, i: b * nb + i
    prv = lambda b, i: b * nb + jnp.maximum(i - 1, 0)
    nxt = lambda b, i: b * nb + jnp.minimum(i + 1, nb - 1)
    kv = lambda rowf, c: pl.BlockSpec((blk, KV_WIDTH), lambda b, i: (rowf(b, i), c))
    return pl.pallas_call(
        _attn_kernel,
        grid=(bsz, nb),
        in_specs=[pl.BlockSpec(memory_space=pltpu.SMEM),
                  pl.BlockSpec((blk, Q_WIDTH), lambda b, i: (cur(b, i), 0)),
                  kv(prv, kcol), kv(cur, kcol), kv(nxt, kcol),
                  kv(prv, vcol), kv(cur, vcol), kv(nxt, vcol),
                  pl.BlockSpec((ATTN_HEADS, blk, 3 * blk), lambda b, i: (0, 0, 0))],
        out_specs=pl.BlockSpec((blk, Q_WIDTH), lambda b, i: (cur(b, i), 0)),
        out_shape=jax.ShapeDtypeStruct((t, Q_WIDTH), F32),
        compiler_params=_params(("parallel", "parallel"), VMEM_LIMIT),
        name="attention",
    )(sink, qkv, qkv, qkv, qkv, qkv, qkv, qkv, bias)


def _t5_bucket(rel):
    n = -rel
    half = N_BUCKETS // 2
    ret = jnp.where(n < 0, half, 0)
    n = jnp.abs(n)
    max_exact = half // 2
    nf = jnp.maximum(n, 1).astype(F32)
    large = max_exact + (jnp.log(nf / max_exact) / jnp.log(MAX_DISTANCE / max_exact)
                         * (half - max_exact)).astype(I32)
    large = jnp.minimum(large, half - 1)
    return ret + jnp.where(n < max_exact, n, large)


def _attn_bias(rel_bias):
    blk = ATTN_BLOCK
    rel = jnp.arange(3 * blk)[None, :] - blk - jnp.arange(blk)[:, None]
    bias = jnp.transpose(rel_bias[_t5_bucket(rel)].astype(F32), (2, 0, 1))
    return jnp.where(jnp.abs(rel) <= WINDOW, bias, NEG)


def _out_proj_kernel(y_ref, x_ref, w_ref, ng_ref, gt_ref, o_ref):
    z = _dot(y_ref[...].astype(BF16), w_ref[...])
    o_ref[...] = x_ref[...] + gt_ref[0] * _rms_norm(z, ng_ref[...])


def _out_proj(y, x2, w_bf16, norm_g, gt, seq, tm=256):
    t, d = x2.shape
    tpb = seq // tm
    row = lambda i: (i, 0)
    c0 = lambda i: (0, 0)
    return pl.pallas_call(
        _out_proj_kernel,
        grid=(t // tm,),
        in_specs=[pl.BlockSpec((tm, d), row),
                  pl.BlockSpec((tm, d), row),
                  pl.BlockSpec((d, d), c0),
                  pl.BlockSpec((1, d), c0),
                  pl.BlockSpec((1, 1, d), lambda i: (i // tpb, 0, 0))],
        out_specs=pl.BlockSpec((tm, d), row),
        out_shape=jax.ShapeDtypeStruct((t, d), F32),
        compiler_params=_params(("parallel",), VMEM_LIMIT),
        name="out_proj",
    )(y, x2, w_bf16, norm_g.reshape(1, d), gt)


def _router_kernel(x_ref, g_ref, sc_ref, sh_ref, rw_ref, rb_ref, idx_o, gate_o, rank_o, cnt_o,
                   cnt_ref):
    @pl.when(pl.program_id(0) == 0)
    def _():
        cnt_ref[...] = jnp.zeros_like(cnt_ref)

    h = _norm_mod(x_ref[...], g_ref[...], sc_ref[0], sh_ref[0])
    tm = h.shape[0]
    ne = N_EXPERTS
    logits = lax.dot_general(rw_ref[...], h, (((1,), (1,)), ((), ())),
                             preferred_element_type=F32, precision=HIGHEST) + rb_ref[...]
    eidx = lax.broadcasted_iota(I32, (ne, tm), 0)
    vals, sels = [], []
    for k in range(TOP_K):
        m = jnp.max(logits, axis=0, keepdims=True)
        idx = jnp.min(jnp.where(logits == m, eidx, ne), axis=0, keepdims=True)
        sel = eidx == idx
        vals.append(m)
        sels.append(sel)
        idx_o[k:k + 1, :] = idx
        logits = jnp.where(sel, -jnp.inf, logits)
    es = [jnp.exp(vk - vals[0]) for vk in vals]
    tot = es[0] + es[1] + es[2] + es[3]
    for k in range(TOP_K):
        gate_o[k:k + 1, :] = es[k] / tot
    onehot = (sels[0] | sels[1] | sels[2] | sels[3])
    before = (lax.broadcasted_iota(I32, (tm, tm), 0) < lax.broadcasted_iota(I32, (tm, tm), 1))
    cum = _dot(onehot.astype(BF16), before.astype(BF16)) + cnt_ref[:, 0:1]
    for k in range(TOP_K):
        rank_o[k:k + 1, :] = jnp.sum(jnp.where(sels[k], cum, 0.0), axis=0,
                                     keepdims=True).astype(I32)
    cnt_ref[...] = cnt_ref[...] + jnp.sum(onehot.astype(F32), axis=1, keepdims=True)
    cnt_o[...] = cnt_ref[...]


def _router(x2, g, sc, sh, router_w_t, router_b, seq, tm=512):
    t, d = x2.shape
    ne = N_EXPERTS
    tpb = seq // tm
    per_b = lambda i: (i // tpb, 0, 0)
    c0 = lambda i: (0, 0)
    col = lambda i: (0, i)
    return pl.pallas_call(
        _router_kernel,
        grid=(t // tm,),
        in_specs=[pl.BlockSpec((tm, d), lambda i: (i, 0)),
                  pl.BlockSpec((1, d), c0),
                  pl.BlockSpec((1, 1, d), per_b),
                  pl.BlockSpec((1, 1, d), per_b),
                  pl.BlockSpec((ne, d), c0),
                  pl.BlockSpec((ne, 1), c0)],
        out_specs=[pl.BlockSpec((TOP_K, tm), col),
                   pl.BlockSpec((TOP_K, tm), col),
                   pl.BlockSpec((TOP_K, tm), col),
                   pl.BlockSpec((ne, 128), c0)],
        out_shape=[jax.ShapeDtypeStruct((TOP_K, t), I32),
                   jax.ShapeDtypeStruct((TOP_K, t), F32),
                   jax.ShapeDtypeStruct((TOP_K, t), I32),
                   jax.ShapeDtypeStruct((ne, 128), F32)],
        scratch_shapes=[pltpu.VMEM((ne, 128), F32)],
        compiler_params=_params(("arbitrary",), VMEM_LIMIT),
        name="router",
    )(x2, g.reshape(1, d), sc, sh, router_w_t, router_b.reshape(ne, 1))


def _dispatch_kernel(x_ref, g_ref, sc_ref, sh_ref, dest_hbm, xs_in, xs_hbm, hbuf, dest_smem, sems):
    del xs_in
    i = pl.program_id(0)
    tm = x_ref.shape[0]
    n_copy = TOP_K * tm
    idx_copy = pltpu.make_async_copy(dest_hbm.at[i], dest_smem, sems.at[0])
    idx_copy.start()
    hbuf[...] = _norm_mod(x_ref[...], g_ref[...], sc_ref[0], sh_ref[0])
    idx_copy.wait()

    def row_copy(j):
        src = hbuf.at[pl.ds(j % tm, 1), :]
        return pltpu.make_async_copy(src, xs_hbm.at[pl.ds(dest_smem[j], 1), :], sems.at[1])

    def issue(j, carry):
        row_copy(j).start()
        return carry

    lax.fori_loop(0, n_copy, issue, 0)

    def drain(j, carry):
        row_copy(j).wait()
        return carry

    lax.fori_loop(0, n_copy, drain, 0)


def _dispatch(x2, g, sc, sh, dest_tiles, xs_init, seq, tm):
    t, d = x2.shape
    tpb = seq // tm
    per_b = lambda i: (i // tpb, 0, 0)
    return pl.pallas_call(
        _dispatch_kernel,
        grid=(t // tm,),
        in_specs=[pl.BlockSpec((tm, d), lambda i: (i, 0)),
                  pl.BlockSpec((1, d), lambda i: (0, 0)),
                  pl.BlockSpec((1, 1, d), per_b),
                  pl.BlockSpec((1, 1, d), per_b),
                  pl.BlockSpec(memory_space=pl.ANY),
                  pl.BlockSpec(memory_space=pl.ANY)],
        out_specs=pl.BlockSpec(memory_space=pl.ANY),
        out_shape=jax.ShapeDtypeStruct(xs_init.shape, F32),
        scratch_shapes=[pltpu.VMEM((tm, d), F32),
                        pltpu.SMEM((TOP_K * tm,), I32),
                        pltpu.SemaphoreType.DMA((2,))],
        input_output_aliases={5: 0},
        compiler_params=_params(("arbitrary",), VMEM_LIMIT),
        name="moe_dispatch",
    )(x2, g.reshape(1, d), sc, sh, dest_tiles, xs_init)


def _expert_kernel(be_ref, first_ref, used_ref, xs_ref, w1_ref, b1_ref, w2_ref, b2_ref, ys_ref,
                   w1p, w2p):
    i = pl.program_id(0)
    f2 = w1_ref.shape[2]
    pw = 256

    @pl.when((first_ref[i] == 1) & (i < used_ref[0]))
    def _():
        r = lax.broadcasted_iota(I32, (pw, pw), 0)
        c = lax.broadcasted_iota(I32, (pw, pw), 1)
        src = jnp.where(c < pw // 2, 2 * c, 2 * (c - pw // 2) + 1)
        perm = (r == src).astype(BF16)
        for j in range(f2 // pw):
            w1p[:, j * pw:(j + 1) * pw] = _dot(w1_ref[0, :, j * pw:(j + 1) * pw].astype(BF16),
                                               perm).astype(BF16)
        w2p[...] = w2_ref[0].astype(BF16)

    @pl.when(i < used_ref[0])
    def _():
        x = xs_ref[...].astype(BF16)
        hcat = _dot(x, w1p[...]) + b1_ref[0]
        acts = []
        for j in range(f2 // pw):
            x_glu = jnp.minimum(hcat[:, j * pw:j * pw + pw // 2], SWIGLU_LIMIT)
            x_lin = jnp.clip(hcat[:, j * pw + pw // 2:(j + 1) * pw], -SWIGLU_LIMIT, SWIGLU_LIMIT)
            acts.append(x_glu * _sigmoid(SWIGLU_ALPHA * x_glu) * (x_lin + 1.0))
        act = jnp.concatenate(acts, axis=1).astype(BF16)
        ys_ref[...] = _dot(act, w2p[...]) + b2_ref[0]

    @pl.when(i >= used_ref[0])
    def _():
        ys_ref[...] = jnp.zeros_like(ys_ref)


def _experts(xs, blk_expert, blk_first, n_used, w1, b1_perm, w2, b2):
    n_slots, d = xs.shape
    ne, _, f2 = w1.shape
    f = w2.shape[1]
    bm = MOE_BLOCK
    grid_spec = pltpu.PrefetchScalarGridSpec(
        num_scalar_prefetch=3,
        grid=(n_slots // bm,),
        in_specs=[pl.BlockSpec((bm, d), lambda i, be, fi, us: (i, 0)),
                  pl.BlockSpec((1, d, f2), lambda i, be, fi, us: (be[i], 0, 0)),
                  pl.BlockSpec((1, 1, f2), lambda i, be, fi, us: (be[i], 0, 0)),
                  pl.BlockSpec((1, f, d), lambda i, be, fi, us: (be[i], 0, 0)),
                  pl.BlockSpec((1, 1, d), lambda i, be, fi, us: (be[i], 0, 0))],
        out_specs=pl.BlockSpec((bm, d), lambda i, be, fi, us: (i, 0)),
        scratch_shapes=[pltpu.VMEM((d, f2), BF16), pltpu.VMEM((f, d), BF16)],
    )
    return pl.pallas_call(
        _expert_kernel,
        grid_spec=grid_spec,
        out_shape=jax.ShapeDtypeStruct((n_slots, d), F32),
        compiler_params=_params(("arbitrary",), VMEM_LIMIT),
        name="moe_experts",
    )(blk_expert, blk_first, n_used, xs, w1, b1_perm.reshape(ne, 1, f2), w2, b2.reshape(ne, 1, d))


def _combine_kernel(x_ref, gate_ref, ng_ref, gt_ref, dest_hbm, ys_hbm, o_ref, ybuf, dest_smem, sems):
    i = pl.program_id(0)
    tm = x_ref.shape[0]
    n_copy = TOP_K * tm
    idx_copy = pltpu.make_async_copy(dest_hbm.at[i], dest_smem, sems.at[0])
    idx_copy.start()
    idx_copy.wait()

    def row_copy(j):
        return pltpu.make_async_copy(ys_hbm.at[pl.ds(dest_smem[j], 1), :],
                                     ybuf.at[pl.ds(j, 1), :], sems.at[1])

    def issue(j, carry):
        row_copy(j).start()
        return carry

    lax.fori_loop(0, n_copy, issue, 0)

    def drain(j, carry):
        row_copy(j).wait()
        return carry

    lax.fori_loop(0, n_copy, drain, 0)
    y = jnp.zeros(x_ref.shape, F32)
    for k in range(TOP_K):
        y = y + gate_ref[:, k:k + 1] * ybuf[k * tm:(k + 1) * tm, :]
    o_ref[...] = x_ref[...] + gt_ref[0] * _rms_norm(y, ng_ref[...])


def _combine(x2, gates_tk, norm_g, gt, dest_tiles, ys, seq, tm):
    t, d = x2.shape
    tpb = seq // tm
    return pl.pallas_call(
        _combine_kernel,
        grid=(t // tm,),
        in_specs=[pl.BlockSpec((tm, d), lambda i: (i, 0)),
                  pl.BlockSpec((tm, TOP_K), lambda i: (i, 0)),
                  pl.BlockSpec((1, d), lambda i: (0, 0)),
                  pl.BlockSpec((1, 1, d), lambda i: (i // tpb, 0, 0)),
                  pl.BlockSpec(memory_space=pl.ANY),
                  pl.BlockSpec(memory_space=pl.ANY)],
        out_specs=pl.BlockSpec((tm, d), lambda i: (i, 0)),
        out_shape=jax.ShapeDtypeStruct((t, d), F32),
        scratch_shapes=[pltpu.VMEM((TOP_K * tm, d), F32),
                        pltpu.SMEM((TOP_K * tm,), I32),
                        pltpu.SemaphoreType.DMA((2,))],
        compiler_params=_params(("arbitrary",), VMEM_LIMIT),
        name="moe_combine",
    )(x2, gates_tk, norm_g.reshape(1, d), gt, dest_tiles, ys)


def _moe(x2, norm_pre, sc, sh, norm_post, gt, router_w, router_b, w1, b1, w2, b2, seq, tm=256):
    t, d = x2.shape
    ne = N_EXPERTS
    bm = MOE_BLOCK
    top_idx, gates, rank, cnt = _router(x2, norm_pre, sc, sh, router_w.T, router_b, seq)
    counts = cnt[:, 0].astype(I32)
    padded = ((counts + bm - 1) // bm) * bm
    padded_end = jnp.cumsum(padded)
    padded_start = padded_end - padded
    dest = padded_start[top_idx] + rank
    n_slots = t * TOP_K + ne * bm
    n_blocks = n_slots // bm
    blk_expert = jnp.minimum(jnp.searchsorted(padded_end, jnp.arange(n_blocks) * bm, side="right"),
                             ne - 1).astype(I32)
    blk_first = jnp.concatenate([jnp.ones((1,), I32),
                                 (blk_expert[1:] != blk_expert[:-1]).astype(I32)])
    n_used = (padded_end[-1:] // bm).astype(I32)
    dest_tiles = dest.reshape(TOP_K, t // tm, tm).transpose(1, 0, 2).reshape(t // tm, TOP_K * tm)
    xs = _dispatch(x2, norm_pre, sc, sh, dest_tiles, jnp.zeros((n_slots, d), F32), seq, tm)
    f2 = w1.shape[2]
    b1_perm = b1.reshape(ne, f2 // 256, 128, 2).transpose(0, 1, 3, 2).reshape(ne, f2)
    ys = _experts(xs, blk_expert, blk_first, n_used, w1, b1_perm, w2, b2)
    return _combine(x2, gates.T, norm_post, gt, dest_tiles, ys, seq, tm)


def kernel(x, c, mod_w, mod_b, norm_mix_pre, norm_mix_post, norm_ffn_pre, norm_ffn_post, ab_w_in, ab_mu_prev, ab_mu_next, rwkv_w0, rwkv_w2, rwkv_a0, rwkv_a2, rwkv_k_k, rwkv_k_a, rwkv_r_k, rwkv_lnx_g, rwkv_lnx_b, rwkv_g2, conv_w, conv_b, conv_gn_g, conv_gn_b, ab_w_out, attn_w_qkv, attn_sink, attn_w_o, rel_bias, router_w, router_b, exp_w1, exp_b1, exp_w2, exp_b2):
    bsz, seq, d = x.shape
    depth = mod_w.shape[0]
    t = bsz * seq
    x2 = x.reshape(t, d)
    mod = _mod_all(c, mod_w, mod_b)
    bias = _attn_bias(rel_bias)
    for layer in range(depth):
        j = layer // 2
        sh1, sc1, gt1, sh2, sc2, gt2 = [m.reshape(bsz, 1, d) for m in jnp.split(mod[layer], 6, axis=-1)]
        if layer % 2 == 0:
            pa, pb = _in_proj(x2, norm_mix_pre[layer], sc1, sh1, ab_w_in[j].astype(BF16),
                              (A_PROJ, 2 * CONV_WIDTH), seq)
            r, v, kk, kd, ad, lw, bonus, g = _rwkv_prep(
                pa, bsz, seq, ab_mu_prev[j], ab_mu_next[j], rwkv_w0[j], rwkv_w2[j], rwkv_a0[j],
                rwkv_a2[j], rwkv_k_k[j], rwkv_k_a[j], rwkv_r_k[j], rwkv_g2[j])
            yf, yb = _rwkv_scan(r, v, kk, kd, ad, lw, bsz, seq)
            cv = _conformer_conv(pb, bsz, seq, conv_w[j], conv_b[j], conv_gn_g[j], conv_gn_b[j])
            x2 = _mix_out(yf, yb, bonus, g, cv, x2, rwkv_lnx_g[j], rwkv_lnx_b[j],
                          ab_w_out[j].astype(BF16), norm_mix_post[layer], gt1, seq)
        else:
            (qkv,) = _in_proj(x2, norm_mix_pre[layer], sc1, sh1, attn_w_qkv[j].astype(BF16),
                              (Q_WIDTH + 2 * KV_WIDTH,), seq)
            y = _attention(qkv, bias, attn_sink[j], bsz, seq)
            x2 = _out_proj(y, x2, attn_w_o[j].astype(BF16), norm_mix_post[layer], gt1, seq)
        x2 = _moe(x2, norm_ffn_pre[layer], sc2, sh2, norm_ffn_post[layer], gt2, router_w[layer],
                  router_b[layer], exp_w1[layer], exp_b1[layer], exp_w2[layer], exp_b2[layer], seq)
    return x2.reshape(bsz, seq, d)
```

```python
import functools

import jax
import jax.numpy as jnp
from jax import lax
from jax.experimental import pallas as pl
from jax.experimental.pallas import tpu as pltpu

F32 = jnp.float32
BF16 = jnp.bfloat16
I32 = jnp.int32
U32 = jnp.uint32
HIGHEST = lax.Precision.HIGHEST

D_MODEL = 1024
HEAD_DIM = 64
RWKV_WIDTH = 512
RWKV_HEADS = RWKV_WIDTH // HEAD_DIM
W_LORA = 64
A_LORA = 64
G_LORA = 128
A_PROJ = 3 * RWKV_WIDTH + W_LORA + A_LORA + G_LORA
CONV_WIDTH = D_MODEL - RWKV_WIDTH
CONV_KERNEL = 31
CONV_PAD = CONV_KERNEL // 2
CONV_GROUPS = 8
ATTN_HEADS = 16
ATTN_KV_HEADS = 2
ATTN_GROUP = ATTN_HEADS // ATTN_KV_HEADS
Q_WIDTH = ATTN_HEADS * HEAD_DIM
KV_WIDTH = ATTN_KV_HEADS * HEAD_DIM
ATTN_BLOCK = 128
WINDOW = 128
N_BUCKETS = 32
MAX_DISTANCE = 128
N_EXPERTS = 32
TOP_K = 4
SWIGLU_LIMIT = 7.0
SWIGLU_ALPHA = 1.702
RMS_EPS = 1e-6
GN_EPS = 1e-5
LNX_EPS = 64e-5
NEG = -1e30

CHUNK = 64
HALO = 16
MOE_BLOCK = 512
ROW_UNROLL = 8
VMEM_LIMIT = 56 * 1024 * 1024


def _params(sem, vmem=None):
    kw = dict(dimension_semantics=sem)
    if vmem is not None:
        kw["vmem_limit_bytes"] = vmem
    return pltpu.CompilerParams(**kw)


def _sigmoid(x):
    return 1.0 / (1.0 + jnp.exp(-x))


def _dot(a, b):
    return jnp.dot(a, b, preferred_element_type=F32)


def _dot_nt(a, b):
    return lax.dot_general(a, b, (((1,), (1,)), ((), ())), preferred_element_type=F32)


def _dot_exact_lhs(a_bf16, x):
    xh = x.astype(BF16)
    xl = (x - xh.astype(F32)).astype(BF16)
    return _dot(a_bf16, xh) + _dot(a_bf16, xl)


def _dot_exact_rhs(x, a_bf16):
    xh = x.astype(BF16)
    xl = (x - xh.astype(F32)).astype(BF16)
    return _dot(xh, a_bf16) + _dot(xl, a_bf16)


def _group_matrix(width, group, value):
    r = lax.broadcasted_iota(I32, (width, width), 0) // group
    c = lax.broadcasted_iota(I32, (width, width), 1) // group
    return jnp.where(r == c, value, 0.0).astype(BF16)


def _rms_norm(x, g):
    ms = jnp.mean(x * x, axis=-1, keepdims=True)
    return x * lax.rsqrt(ms + RMS_EPS) * g


def _norm_mod(x, g, sc, sh):
    return _rms_norm(x, g) * (1.0 + sc) + sh


def _mod_kernel(c_ref, w_ref, b_ref, o_ref):
    c = c_ref[...]
    ca = c * _sigmoid(c)
    o_ref[0] = jnp.dot(ca, w_ref[0], preferred_element_type=F32, precision=HIGHEST) + b_ref[0]


def _mod_all(c, mod_w, mod_b):
    depth, d, n = mod_w.shape
    bsz = c.shape[0]
    tn = 1536
    return pl.pallas_call(
        _mod_kernel,
        grid=(depth, n // tn),
        in_specs=[pl.BlockSpec((bsz, d), lambda l, j: (0, 0)),
                  pl.BlockSpec((1, d, tn), lambda l, j: (l, 0, j)),
                  pl.BlockSpec((1, 1, tn), lambda l, j: (l, 0, j))],
        out_specs=pl.BlockSpec((1, bsz, tn), lambda l, j: (l, 0, j)),
        out_shape=jax.ShapeDtypeStruct((depth, bsz, n), F32),
        compiler_params=_params(("parallel", "parallel"), VMEM_LIMIT),
        name="mod",
    )(c, mod_w, mod_b.reshape(depth, 1, n))


def _in_proj_kernel(x_ref, g_ref, sc_ref, sh_ref, w_ref, *o_refs):
    h = _norm_mod(x_ref[...], g_ref[...], sc_ref[0], sh_ref[0])
    p = _dot(h.astype(BF16), w_ref[...])
    off = 0
    for o_ref in o_refs:
        wdt = o_ref.shape[1]
        o_ref[...] = p[:, off:off + wdt]
        off += wdt


def _in_proj(x2, g, sc, sh, w_bf16, splits, seq, tm=256):
    t, d = x2.shape
    n = w_bf16.shape[1]
    tpb = seq // tm
    row = lambda i: (i, 0)
    per_b = lambda i: (i // tpb, 0, 0)
    return pl.pallas_call(
        _in_proj_kernel,
        grid=(t // tm,),
        in_specs=[pl.BlockSpec((tm, d), row),
                  pl.BlockSpec((1, d), lambda i: (0, 0)),
                  pl.BlockSpec((1, 1, d), per_b),
                  pl.BlockSpec((1, 1, d), per_b),
                  pl.BlockSpec((d, n), lambda i: (0, 0))],
        out_specs=[pl.BlockSpec((tm, s), row) for s in splits],
        out_shape=[jax.ShapeDtypeStruct((t, s), F32) for s in splits],
        compiler_params=_params(("parallel",), VMEM_LIMIT),
        name="in_proj",
    )(x2, g.reshape(1, d), sc, sh, w_bf16)


def _rwkv_prep_kernel(pm_ref, pp_ref, pn_ref, mup_ref, mun_ref, w0_ref, w2_ref, a0_ref, a2_ref,
                      kk_ref, ka_ref, rk_ref, g2_ref,
                      r_o, v_o, kk_o, kd_o, ad_o, lw_o, bonus_o, g_o):
    i = pl.program_id(1)
    n = pl.num_programs(1)
    p = pm_ref[...]
    ts = p.shape[0]
    rows = lax.broadcasted_iota(I32, (ts, 1), 0)
    prev_row = jnp.where(i > 0, pp_ref[HALO - 1:HALO, :], 0.0)
    next_row = jnp.where(i < n - 1, pn_ref[0:1, :], 0.0)
    prev = jnp.where(rows == 0, prev_row, pltpu.roll(p, 1, 0))
    nxt = jnp.where(rows == ts - 1, next_row, pltpu.roll(p, ts - 1, 0))
    pa = p + mup_ref[...] * (prev - p) + mun_ref[...] * (nxt - p)

    w = RWKV_WIDTH
    r = pa[:, 0:w]
    k = pa[:, w:2 * w]
    v = pa[:, 2 * w:3 * w]
    w_lo = pa[:, 3 * w:3 * w + W_LORA]
    a_lo = pa[:, 3 * w + W_LORA:3 * w + W_LORA + A_LORA]
    g_lo = pa[:, 3 * w + W_LORA + A_LORA:]

    ones = _group_matrix(w, HEAD_DIM, 1.0)
    kk = k * kk_ref[...]
    ss = _dot_exact_rhs(kk * kk, ones)
    kk = kk / jnp.maximum(jnp.sqrt(ss), 1e-12)
    tw = jnp.tanh(w_lo)
    r_o[...] = r
    v_o[...] = v
    kk_o[...] = kk
    bonus = jnp.zeros_like(r)
    for d in range(2):
        z = w0_ref[d] + jnp.dot(tw, w2_ref[d], preferred_element_type=F32, precision=HIGHEST)
        nz = -z
        softplus = jnp.maximum(nz, 0.0) + jnp.log(1.0 + jnp.exp(-jnp.abs(nz)))
        w_log = -softplus - 0.5
        lw_o[d] = -jnp.exp(w_log)
        a = _sigmoid(a0_ref[d] + jnp.dot(a_lo, a2_ref[d], preferred_element_type=F32,
                                         precision=HIGHEST))
        k_d = k * (1.0 + (a - 1.0) * ka_ref[...])
        ad_o[d] = a
        kd_o[d] = k_d
        bonus = bonus + _dot_exact_rhs(r * k_d * rk_ref[...], ones) * v
    bonus_o[...] = bonus
    g_o[...] = jnp.dot(_sigmoid(g_lo), g2_ref[...], preferred_element_type=F32, precision=HIGHEST)


def _rwkv_prep(pa, bsz, seq, mu_prev, mu_next, w0, w2, a0, a2, k_k, k_a, r_k, g2, ts=256):
    t = pa.shape[0]
    w = RWKV_WIDTH
    nt = seq // ts
    hb = ts // HALO
    main = lambda b, i: (b * nt + i, 0)
    prev = lambda b, i: (jnp.maximum((b * nt + i) * hb - 1, 0), 0)
    nxt = lambda b, i: (jnp.minimum((b * nt + i + 1) * hb, t // HALO - 1), 0)
    c0 = lambda b, i: (0, 0)
    c3 = lambda b, i: (0, 0, 0)
    dmain = lambda b, i: (0, b * nt + i, 0)
    tok = jax.ShapeDtypeStruct((t, w), F32)
    tok2 = jax.ShapeDtypeStruct((2, t, w), F32)
    return pl.pallas_call(
        _rwkv_prep_kernel,
        grid=(bsz, nt),
        in_specs=[pl.BlockSpec((ts, A_PROJ), main),
                  pl.BlockSpec((HALO, A_PROJ), prev),
                  pl.BlockSpec((HALO, A_PROJ), nxt),
                  pl.BlockSpec((1, A_PROJ), c0),
                  pl.BlockSpec((1, A_PROJ), c0),
                  pl.BlockSpec((2, 1, w), c3),
                  pl.BlockSpec((2, W_LORA, w), c3),
                  pl.BlockSpec((2, 1, w), c3),
                  pl.BlockSpec((2, A_LORA, w), c3),
                  pl.BlockSpec((1, w), c0),
                  pl.BlockSpec((1, w), c0),
                  pl.BlockSpec((1, w), c0),
                  pl.BlockSpec((G_LORA, w), c0)],
        out_specs=[pl.BlockSpec((ts, w), main)] * 3
                  + [pl.BlockSpec((2, ts, w), dmain)] * 3
                  + [pl.BlockSpec((ts, w), main)] * 2,
        out_shape=[tok, tok, tok, tok2, tok2, tok2, tok, tok],
        compiler_params=_params(("parallel", "parallel"), VMEM_LIMIT),
        name="rwkv_prep",
    )(pa, pa, pa, mu_prev.reshape(1, -1), mu_next.reshape(1, -1), w0.reshape(2, 1, w), w2,
      a0.reshape(2, 1, w), a2, k_k.reshape(1, w), k_a.reshape(1, w), r_k.reshape(1, w), g2)


def _wkv_scaled(r, kd, v, kk, a, lw, reverse):
    c = r.shape[0]
    ri = lax.broadcasted_iota(I32, (c, c), 0)
    ci = lax.broadcasted_iota(I32, (c, c), 1)
    if reverse:
        incl, strict, last = ci >= ri, ci > ri, 0
    else:
        incl, strict, last = ci <= ri, ci < ri, c - 1
    cum = _dot_exact_lhs(incl.astype(BF16), lw)
    cum_last = cum[last:last + 1, :]
    w_end = jnp.exp(cum_last - cum)
    w_inv = jnp.exp(-cum)
    b = kk * a
    return dict(rt=r * jnp.exp(cum), kt=kd * w_inv, bt=b * w_inv, at=-kk * jnp.exp(cum - lw),
                bk_t=jnp.concatenate([b * w_end, kd * w_end], axis=0).T,
                w_c=jnp.exp(cum_last), v=v, incl=incl, strict=strict)


def _wkv_chunks(ops, states):
    c = ops[0]["rt"].shape[0]
    hd = HEAD_DIM
    n_heads = ops[0]["rt"].shape[1] // hd
    streams = [(d, h) for d in range(len(ops)) for h in range(n_heads)]
    ri = lax.broadcasted_iota(I32, (c, c), 0)
    ci = lax.broadcasted_iota(I32, (c, c), 1)
    eye = (ri == ci).astype(F32)
    zeros = jnp.zeros((c, hd), F32)

    def head(d, h, name):
        return ops[d][name][:, h * hd:(h + 1) * hd]

    ll = [_dot_nt(jnp.concatenate([head(d, h, "at"), head(d, h, "rt")], axis=0),
                  jnp.concatenate([head(d, h, "bt"), head(d, h, "kt")], axis=0))
          for d, h in streams]
    l_ab = [jnp.where(ops[d]["strict"], m[:c, :c], 0.0) for (d, h), m in zip(streams, ll)]
    l_ak = [jnp.where(ops[d]["strict"], m[:c, c:], 0.0) for (d, h), m in zip(streams, ll)]
    l_rr = [jnp.concatenate([jnp.where(ops[d]["incl"], m[c:, :c], 0.0),
                             jnp.where(ops[d]["incl"], m[c:, c:], 0.0)], axis=1)
            for (d, h), m in zip(streams, ll)]
    tinv = [eye + m for m in l_ab]
    lp = l_ab
    for _ in range(max(1, (c - 1).bit_length() - 1)):
        lp = [_dot(m, m) for m in lp]
        tinv = [t + _dot(m, t) for m, t in zip(lp, tinv)]
    lakv = [_dot(m, head(d, h, "v")) for (d, h), m in zip(streams, l_ak)]
    x = [_dot(t, jnp.concatenate([head(d, h, "at"), u], axis=1))
         for (d, h), t, u in zip(streams, tinv, lakv)]
    o2 = [_dot(jnp.concatenate([lr, ops[d]["bk_t"][h * hd:(h + 1) * hd, :]], axis=0),
               jnp.concatenate([xx, jnp.concatenate([zeros, head(d, h, "v")], axis=1)], axis=0))
          for (d, h), lr, xx in zip(streams, l_rr, x)]
    o3 = [_dot(jnp.concatenate([head(d, h, "rt") + m[:c, :hd],
                                eye[:hd, :hd] * head(d, h, "w_c") + m[c:, :hd]], axis=0),
               states[d][:, h * hd:(h + 1) * hd])
          for (d, h), m in zip(streams, o2)]
    ys, sts = [], []
    for d in range(len(ops)):
        sel = [(m2, m3) for (dd, h), m2, m3 in zip(streams, o2, o3) if dd == d]
        ys.append(jnp.concatenate([m3[:c] + m2[:c, hd:] for m2, m3 in sel], axis=1))
        sts.append(jnp.concatenate([m3[c:] + m2[c:, hd:] for m2, m3 in sel], axis=1))
    return ys, sts


def _rwkv_scan_kernel(rf, vf, kkf, kdf, adf, lwf, rb, vb, kkb, kdb, adb, lwb, yf_o, yb_o, st_ref):
    @pl.when(pl.program_id(1) == 0)
    def _():
        st_ref[...] = jnp.zeros_like(st_ref)

    ops = [_wkv_scaled(rf[...], kdf[0], vf[...], kkf[...], adf[0], lwf[0], False),
           _wkv_scaled(rb[...], kdb[0], vb[...], kkb[...], adb[0], lwb[0], True)]
    ys, sts = _wkv_chunks(ops, [st_ref[0], st_ref[1]])
    yf_o[...] = ys[0]
    yb_o[...] = ys[1]
    st_ref[0] = sts[0]
    st_ref[1] = sts[1]


def _rwkv_scan(r, v, kk, kd, ad, lw, bsz, seq):
    t, w = r.shape
    nc = seq // CHUNK
    fwd = lambda b, i: (b * nc + i, 0)
    bwd = lambda b, i: (b * nc + nc - 1 - i, 0)
    fwd0 = lambda b, i: (0, b * nc + i, 0)
    bwd1 = lambda b, i: (1, b * nc + nc - 1 - i, 0)
    tok = pl.BlockSpec((CHUNK, w), fwd)
    tokb = pl.BlockSpec((CHUNK, w), bwd)
    dir0 = pl.BlockSpec((1, CHUNK, w), fwd0)
    dir1 = pl.BlockSpec((1, CHUNK, w), bwd1)
    return pl.pallas_call(
        _rwkv_scan_kernel,
        grid=(bsz, nc),
        in_specs=[tok, tok, tok, dir0, dir0, dir0, tokb, tokb, tokb, dir1, dir1, dir1],
        out_specs=[tok, tokb],
        out_shape=[jax.ShapeDtypeStruct((t, w), F32)] * 2,
        scratch_shapes=[pltpu.VMEM((2, HEAD_DIM, w), F32)],
        compiler_params=_params(("parallel", "arbitrary"), VMEM_LIMIT),
        name="rwkv_scan",
    )(r, v, kk, kd, ad, lw, r, v, kk, kd, ad, lw)


def _conv_kernel(pm_ref, pp_ref, pn_ref, cw_ref, cb_ref, g_ref, b_ref, o_ref, hbuf):
    i = pl.program_id(1)
    n = pl.num_programs(1)
    ts = pm_ref.shape[0]
    cwid = CONV_WIDTH

    def glu(p):
        return p[:, :cwid] * _sigmoid(p[:, cwid:])

    hbuf[0:HALO, :] = jnp.where(i > 0, glu(pp_ref[...]), 0.0)
    hbuf[HALO:HALO + ts, :] = glu(pm_ref[...])
    hbuf[HALO + ts:2 * HALO + ts, :] = jnp.where(i < n - 1, glu(pn_ref[...]), 0.0)
    acc = jnp.zeros((ts, cwid), F32)
    for k in range(CONV_KERNEL):
        off = HALO - CONV_PAD + k
        acc = acc + cw_ref[k:k + 1, :] * hbuf[off:off + ts, :]
    acc = acc + cb_ref[...]
    avg = _group_matrix(cwid, cwid // CONV_GROUPS, 1.0 / (cwid // CONV_GROUPS))
    mu = _dot_exact_rhs(acc, avg)
    xc = acc - mu
    var = _dot_exact_rhs(xc * xc, avg)
    y = xc * lax.rsqrt(var + GN_EPS) * g_ref[...] + b_ref[...]
    o_ref[...] = y * _sigmoid(y)


def _conformer_conv(pb, bsz, seq, conv_w, conv_b, gn_g, gn_b, ts=512):
    t = pb.shape[0]
    cwid = CONV_WIDTH
    nt = seq // ts
    hb = ts // HALO
    main = lambda b, i: (b * nt + i, 0)
    prev = lambda b, i: (jnp.maximum((b * nt + i) * hb - 1, 0), 0)
    nxt = lambda b, i: (jnp.minimum((b * nt + i + 1) * hb, t // HALO - 1), 0)
    c0 = lambda b, i: (0, 0)
    return pl.pallas_call(
        _conv_kernel,
        grid=(bsz, nt),
        in_specs=[pl.BlockSpec((ts, 2 * cwid), main),
                  pl.BlockSpec((HALO, 2 * cwid), prev),
                  pl.BlockSpec((HALO, 2 * cwid), nxt),
                  pl.BlockSpec((CONV_KERNEL, cwid), c0),
                  pl.BlockSpec((1, cwid), c0),
                  pl.BlockSpec((1, cwid), c0),
                  pl.BlockSpec((1, cwid), c0)],
        out_specs=pl.BlockSpec((ts, cwid), main),
        out_shape=jax.ShapeDtypeStruct((t, cwid), F32),
        scratch_shapes=[pltpu.VMEM((ts + 2 * HALO, cwid), F32)],
        compiler_params=_params(("parallel", "parallel"), VMEM_LIMIT),
        name="conformer_conv",
    )(pb, pb, pb, conv_w, conv_b.reshape(1, cwid), gn_g.reshape(1, cwid), gn_b.reshape(1, cwid))


def _mix_out_kernel(yf_ref, yb_ref, bonus_ref, g_ref, cv_ref, x_ref, lg_ref, lb_ref, w_ref,
                    ng_ref, gt_ref, o_ref):
    w = RWKV_WIDTH
    wkv = yf_ref[...] + yb_ref[...]
    avg = _group_matrix(w, HEAD_DIM, 1.0 / HEAD_DIM)
    mu = _dot_exact_rhs(wkv, avg)
    xc = wkv - mu
    var = _dot_exact_rhs(xc * xc, avg)
    y = xc * lax.rsqrt(var + LNX_EPS) * lg_ref[...] + lb_ref[...] + bonus_ref[...]
    ya = y * g_ref[...]
    cat = jnp.concatenate([ya.astype(BF16), cv_ref[...].astype(BF16)], axis=1)
    z = _dot(cat, w_ref[...])
    o_ref[...] = x_ref[...] + gt_ref[0] * _rms_norm(z, ng_ref[...])


def _mix_out(yf, yb, bonus, g, cv, x2, lnx_g, lnx_b, w_out_bf16, norm_g, gt, seq, tm=256):
    t, d = x2.shape
    w = RWKV_WIDTH
    tpb = seq // tm
    row = lambda i: (i, 0)
    c0 = lambda i: (0, 0)
    half = pl.BlockSpec((tm, w), row)
    return pl.pallas_call(
        _mix_out_kernel,
        grid=(t // tm,),
        in_specs=[half, half, half, half, half,
                  pl.BlockSpec((tm, d), row),
                  pl.BlockSpec((1, w), c0),
                  pl.BlockSpec((1, w), c0),
                  pl.BlockSpec((d, d), c0),
                  pl.BlockSpec((1, d), c0),
                  pl.BlockSpec((1, 1, d), lambda i: (i // tpb, 0, 0))],
        out_specs=pl.BlockSpec((tm, d), row),
        out_shape=jax.ShapeDtypeStruct((t, d), F32),
        compiler_params=_params(("parallel",), VMEM_LIMIT),
        name="mix_out",
    )(yf, yb, bonus, g, cv, x2, lnx_g.reshape(1, w), lnx_b.reshape(1, w), w_out_bf16,
      norm_g.reshape(1, d), gt)


def _attn_kernel(sink_ref, q_ref, kp_ref, kc_ref, kn_ref, vp_ref, vc_ref, vn_ref, bias_ref, o_ref):
    i = pl.program_id(1)
    n = pl.num_programs(1)
    blk = ATTN_BLOCK
    hd = HEAD_DIM
    k = jnp.concatenate([kp_ref[...], kc_ref[...], kn_ref[...]], axis=0).astype(BF16)
    v = jnp.concatenate([vp_ref[...], vc_ref[...], vn_ref[...]], axis=0).astype(BF16)
    col = lax.broadcasted_iota(I32, (1, 3 * blk), 1)
    valid = ((col >= blk) | (i > 0)) & ((col < 2 * blk) | (i < n - 1))
    scale = hd ** -0.5
    kvs = range(ATTN_KV_HEADS)
    qs = [jnp.concatenate([q_ref[:, (kvh * ATTN_GROUP + g) * hd:(kvh * ATTN_GROUP + g + 1) * hd]
                           for g in range(ATTN_GROUP)], axis=0).astype(BF16) for kvh in kvs]
    scs = [_dot_nt(qs[kvh], k[:, kvh * hd:(kvh + 1) * hd]) for kvh in kvs]
    ps, denoms = [], []
    for kvh in kvs:
        sc = jnp.where(valid, scs[kvh] * scale + bias_ref[kvh], NEG)
        sink = sink_ref[kvh]
        m = jnp.maximum(jnp.max(sc, axis=-1, keepdims=True), sink)
        p = jnp.exp(sc - m)
        denoms.append(jnp.sum(p, axis=-1, keepdims=True) + jnp.exp(sink - m))
        ps.append(p.astype(BF16))
    for kvh in kvs:
        o = _dot(ps[kvh], v[:, kvh * hd:(kvh + 1) * hd]) / denoms[kvh]
        for g in range(ATTN_GROUP):
            h = kvh * ATTN_GROUP + g
            o_ref[:, h * hd:(h + 1) * hd] = o[g * blk:(g + 1) * blk, :]


def _attention(qkv, bias, sink, bsz, seq):
    t = qkv.shape[0]
    rows = ATTN_GROUP * ATTN_BLOCK
    bias = bias.reshape(ATTN_KV_HEADS, rows, 3 * ATTN_BLOCK)
    sink = jnp.repeat(sink.astype(F32), ATTN_BLOCK).reshape(ATTN_KV_HEADS, rows, 1)
    blk = ATTN_BLOCK
    nb = seq // blk
    kcol = Q_WIDTH // KV_WIDTH
    vcol = kcol + 1
    cur = lambda b, i: b * nb + i
    prv = lambda b, i: b * nb + jnp.maximum(i - 1, 0)
    nxt = lambda b, i: b * nb + jnp.minimum(i + 1, nb - 1)
    kv = lambda rowf, c: pl.BlockSpec((blk, KV_WIDTH), lambda b, i: (rowf(b, i), c))
    return pl.pallas_call(
        _attn_kernel,
        grid=(bsz, nb),
        in_specs=[pl.BlockSpec((ATTN_KV_HEADS, rows, 1), lambda b, i: (0, 0, 0)),
                  pl.BlockSpec((blk, Q_WIDTH), lambda b, i: (cur(b, i), 0)),
                  kv(prv, kcol), kv(cur, kcol), kv(nxt, kcol),
                  kv(prv, vcol), kv(cur, vcol), kv(nxt, vcol),
                  pl.BlockSpec((ATTN_KV_HEADS, rows, 3 * blk), lambda b, i: (0, 0, 0))],
        out_specs=pl.BlockSpec((blk, Q_WIDTH), lambda b, i: (cur(b, i), 0)),
        out_shape=jax.ShapeDtypeStruct((t, Q_WIDTH), F32),
        compiler_params=_params(("parallel", "parallel"), VMEM_LIMIT),
        name="attention",
    )(sink, qkv, qkv, qkv, qkv, qkv, qkv, qkv, bias)


def _t5_bucket(rel):
    n = -rel
    half = N_BUCKETS // 2
    ret = jnp.where(n < 0, half, 0)
    n = jnp.abs(n)
    max_exact = half // 2
    nf = jnp.maximum(n, 1).astype(F32)
    large = max_exact + (jnp.log(nf / max_exact) / jnp.log(MAX_DISTANCE / max_exact)
                         * (half - max_exact)).astype(I32)
    large = jnp.minimum(large, half - 1)
    return ret + jnp.where(n < max_exact, n, large)


def _attn_bias(rel_bias):
    blk = ATTN_BLOCK
    rel = jnp.arange(3 * blk)[None, :] - blk - jnp.arange(blk)[:, None]
    bias = jnp.transpose(rel_bias[_t5_bucket(rel)].astype(F32), (2, 0, 1))
    return jnp.where(jnp.abs(rel) <= WINDOW, bias, NEG)


def _out_proj_kernel(y_ref, x_ref, w_ref, ng_ref, gt_ref, o_ref):
    z = _dot(y_ref[...].astype(BF16), w_ref[...])
    o_ref[...] = x_ref[...] + gt_ref[0] * _rms_norm(z, ng_ref[...])


def _out_proj(y, x2, w_bf16, norm_g, gt, seq, tm=256):
    t, d = x2.shape
    tpb = seq // tm
    row = lambda i: (i, 0)
    c0 = lambda i: (0, 0)
    return pl.pallas_call(
        _out_proj_kernel,
        grid=(t // tm,),
        in_specs=[pl.BlockSpec((tm, d), row),
                  pl.BlockSpec((tm, d), row),
                  pl.BlockSpec((d, d), c0),
                  pl.BlockSpec((1, d), c0),
                  pl.BlockSpec((1, 1, d), lambda i: (i // tpb, 0, 0))],
        out_specs=pl.BlockSpec((tm, d), row),
        out_shape=jax.ShapeDtypeStruct((t, d), F32),
        compiler_params=_params(("parallel",), VMEM_LIMIT),
        name="out_proj",
    )(y, x2, w_bf16, norm_g.reshape(1, d), gt)


def _router_kernel(x_ref, g_ref, sc_ref, sh_ref, rw_ref, rb_ref, idx_o, gate_o, rank_o, cnt_o,
                   cnt_ref):
    @pl.when(pl.program_id(0) == 0)
    def _():
        cnt_ref[...] = jnp.zeros_like(cnt_ref)

    h = _norm_mod(x_ref[...], g_ref[...], sc_ref[0], sh_ref[0])
    tm = h.shape[0]
    ne = N_EXPERTS
    logits = lax.dot_general(rw_ref[...], h, (((1,), (1,)), ((), ())),
                             preferred_element_type=F32, precision=HIGHEST) + rb_ref[...]
    eidx = lax.broadcasted_iota(I32, (ne, tm), 0)
    vals, sels = [], []
    for k in range(TOP_K):
        m = jnp.max(logits, axis=0, keepdims=True)
        idx = jnp.min(jnp.where(logits == m, eidx, ne), axis=0, keepdims=True)
        sel = eidx == idx
        vals.append(m)
        sels.append(sel)
        idx_o[k:k + 1, :] = idx
        logits = jnp.where(sel, -jnp.inf, logits)
    es = [jnp.exp(vk - vals[0]) for vk in vals]
    tot = es[0] + es[1] + es[2] + es[3]
    for k in range(TOP_K):
        gate_o[k:k + 1, :] = es[k] / tot
    onehot = (sels[0] | sels[1] | sels[2] | sels[3])
    before = (lax.broadcasted_iota(I32, (tm, tm), 0) < lax.broadcasted_iota(I32, (tm, tm), 1))
    cum = _dot(onehot.astype(BF16), before.astype(BF16)) + cnt_ref[:, 0:1]
    for k in range(TOP_K):
        rank_o[k:k + 1, :] = jnp.sum(jnp.where(sels[k], cum, 0.0), axis=0,
                                     keepdims=True).astype(I32)
    cnt_ref[...] = cnt_ref[...] + jnp.sum(onehot.astype(F32), axis=1, keepdims=True)
    cnt_o[...] = cnt_ref[...]


def _router(x2, g, sc, sh, router_w_t, router_b, seq, tm=512):
    t, d = x2.shape
    ne = N_EXPERTS
    tpb = seq // tm
    per_b = lambda i: (i // tpb, 0, 0)
    c0 = lambda i: (0, 0)
    col = lambda i: (0, i)
    return pl.pallas_call(
        _router_kernel,
        grid=(t // tm,),
        in_specs=[pl.BlockSpec((tm, d), lambda i: (i, 0)),
                  pl.BlockSpec((1, d), c0),
                  pl.BlockSpec((1, 1, d), per_b),
                  pl.BlockSpec((1, 1, d), per_b),
                  pl.BlockSpec((ne, d), c0),
                  pl.BlockSpec((ne, 1), c0)],
        out_specs=[pl.BlockSpec((TOP_K, tm), col),
                   pl.BlockSpec((TOP_K, tm), col),
                   pl.BlockSpec((TOP_K, tm), col),
                   pl.BlockSpec((ne, 128), c0)],
        out_shape=[jax.ShapeDtypeStruct((TOP_K, t), I32),
                   jax.ShapeDtypeStruct((TOP_K, t), F32),
                   jax.ShapeDtypeStruct((TOP_K, t), I32),
                   jax.ShapeDtypeStruct((ne, 128), F32)],
        scratch_shapes=[pltpu.VMEM((ne, 128), F32)],
        compiler_params=_params(("arbitrary",), VMEM_LIMIT),
        name="router",
    )(x2, g.reshape(1, d), sc, sh, router_w_t, router_b.reshape(ne, 1))


def _pack_rows(h):
    half = h.shape[1] // 2
    hi = lax.bitcast_convert_type(h[:, :half].astype(BF16).astype(F32), U32)
    lo = lax.bitcast_convert_type(h[:, half:].astype(BF16).astype(F32), U32)
    return (hi & jnp.uint32(0xFFFF0000)) | (lo >> 16)


def _unpack_rows(p):
    hi = lax.bitcast_convert_type(p & jnp.uint32(0xFFFF0000), F32)
    lo = lax.bitcast_convert_type(p << 16, F32)
    return jnp.concatenate([hi, lo], axis=1)


def _tile_indices(dest_hbm, dest_smem, sem):
    i = pl.program_id(0)
    n = pl.num_programs(0)
    slot = i % 2

    def fetch(step, s):
        return pltpu.make_async_copy(dest_hbm.at[step], dest_smem.at[s], sem)

    @pl.when(i == 0)
    def _():
        fetch(0, 0).start()

    fetch(i, slot).wait()

    @pl.when(i + 1 < n)
    def _():
        fetch(i + 1, 1 - slot).start()

    return slot


def _dispatch_kernel(x_ref, g_ref, sc_ref, sh_ref, dest_hbm, xs_in, xs_hbm, hbuf, dest_smem, sems):
    del xs_in
    tm = x_ref.shape[0]
    hbuf[...] = _pack_rows(_norm_mod(x_ref[...], g_ref[...], sc_ref[0], sh_ref[0]))
    slot = _tile_indices(dest_hbm, dest_smem, sems.at[0])

    def issue(grp, carry):
        for u in range(ROW_UNROLL):
            row = grp * ROW_UNROLL + u
            for k in range(TOP_K):
                pltpu.make_async_copy(hbuf.at[pl.ds(row, 1), :],
                                      xs_hbm.at[pl.ds(dest_smem[slot, k * tm + row], 1), :],
                                      sems.at[1]).start()
        return carry

    lax.fori_loop(0, tm // ROW_UNROLL, issue, 0)

    def drain(grp, carry):
        for u in range(TOP_K * ROW_UNROLL):
            pltpu.make_async_copy(hbuf.at[pl.ds(0, 1), :], xs_hbm.at[pl.ds(0, 1), :],
                                  sems.at[1]).wait()
        return carry

    lax.fori_loop(0, tm // ROW_UNROLL, drain, 0)


def _dispatch(x2, g, sc, sh, dest_tiles, xs_init, seq, tm):
    t, d = x2.shape
    tpb = seq // tm
    per_b = lambda i: (i // tpb, 0, 0)
    return pl.pallas_call(
        _dispatch_kernel,
        grid=(t // tm,),
        in_specs=[pl.BlockSpec((tm, d), lambda i: (i, 0)),
                  pl.BlockSpec((1, d), lambda i: (0, 0)),
                  pl.BlockSpec((1, 1, d), per_b),
                  pl.BlockSpec((1, 1, d), per_b),
                  pl.BlockSpec(memory_space=pl.ANY),
                  pl.BlockSpec(memory_space=pl.ANY)],
        out_specs=pl.BlockSpec(memory_space=pl.ANY),
        out_shape=jax.ShapeDtypeStruct(xs_init.shape, U32),
        scratch_shapes=[pltpu.VMEM((tm, d // 2), U32),
                        pltpu.SMEM((2, TOP_K * tm), I32),
                        pltpu.SemaphoreType.DMA((2,))],
        input_output_aliases={5: 0},
        compiler_params=_params(("arbitrary",), VMEM_LIMIT),
        name="moe_dispatch",
    )(x2, g.reshape(1, d), sc, sh, dest_tiles, xs_init)


def _expert_kernel(be_ref, first_ref, used_ref, xs_ref, w1_ref, b1_ref, w2_ref, b2_ref, ys_ref,
                   w1p, w2p):
    i = pl.program_id(0)
    f2 = w1_ref.shape[3]
    pw = 256

    @pl.when((first_ref[i] == 1) & (i < used_ref[0]))
    def _():
        r = lax.broadcasted_iota(I32, (pw, pw), 0)
        c = lax.broadcasted_iota(I32, (pw, pw), 1)
        src = jnp.where(c < pw // 2, 2 * c, 2 * (c - pw // 2) + 1)
        perm = (r == src).astype(BF16)
        for j in range(f2 // pw):
            w1p[:, j * pw:(j + 1) * pw] = _dot(w1_ref[0, 0, :, j * pw:(j + 1) * pw].astype(BF16),
                                               perm).astype(BF16)
        w2p[...] = w2_ref[0, 0].astype(BF16)

    @pl.when(i < used_ref[0])
    def _():
        x = _unpack_rows(xs_ref[...]).astype(BF16)
        hcat = _dot(x, w1p[...]) + b1_ref[0, 0]
        acts = []
        for j in range(f2 // pw):
            x_glu = jnp.minimum(hcat[:, j * pw:j * pw + pw // 2], SWIGLU_LIMIT)
            x_lin = jnp.clip(hcat[:, j * pw + pw // 2:(j + 1) * pw], -SWIGLU_LIMIT, SWIGLU_LIMIT)
            acts.append(x_glu * _sigmoid(SWIGLU_ALPHA * x_glu) * (x_lin + 1.0))
        act = jnp.concatenate(acts, axis=1).astype(BF16)
        ys_ref[...] = _pack_rows(_dot(act, w2p[...]) + b2_ref[0, 0])

    @pl.when(i >= used_ref[0])
    def _():
        ys_ref[...] = jnp.zeros_like(ys_ref)


def _experts(xs, blk_expert, blk_first, n_used, layer, w1, b1_perm, w2, b2):
    n_slots, dp = xs.shape
    _, ne, d, f2 = w1.shape
    f = w2.shape[2]
    bm = MOE_BLOCK
    wmap = lambda i, be, fi, us: (layer, be[i], 0, 0)
    grid_spec = pltpu.PrefetchScalarGridSpec(
        num_scalar_prefetch=3,
        grid=(n_slots // bm,),
        in_specs=[pl.BlockSpec((bm, dp), lambda i, be, fi, us: (i, 0)),
                  pl.BlockSpec((1, 1, d, f2), wmap),
                  pl.BlockSpec((1, 1, 1, f2), wmap),
                  pl.BlockSpec((1, 1, f, d), wmap),
                  pl.BlockSpec((1, 1, 1, d), wmap)],
        out_specs=pl.BlockSpec((bm, dp), lambda i, be, fi, us: (i, 0)),
        scratch_shapes=[pltpu.VMEM((d, f2), BF16), pltpu.VMEM((f, d), BF16)],
    )
    depth = w1.shape[0]
    return pl.pallas_call(
        _expert_kernel,
        grid_spec=grid_spec,
        out_shape=jax.ShapeDtypeStruct((n_slots, dp), U32),
        compiler_params=_params(("arbitrary",), VMEM_LIMIT),
        name="moe_experts",
    )(blk_expert, blk_first, n_used, xs, w1, b1_perm.reshape(depth, ne, 1, f2), w2,
      b2.reshape(depth, ne, 1, d))


def _combine_kernel(x_ref, gate_ref, ng_ref, gt_ref, dest_hbm, ys_hbm, o_ref, ybuf, dest_smem, sems):
    tm = x_ref.shape[0]
    n_copy = TOP_K * tm
    slot = _tile_indices(dest_hbm, dest_smem, sems.at[0])

    def issue(grp, carry):
        for u in range(TOP_K * ROW_UNROLL):
            j = grp * (TOP_K * ROW_UNROLL) + u
            pltpu.make_async_copy(ys_hbm.at[pl.ds(dest_smem[slot, j], 1), :],
                                  ybuf.at[pl.ds(j, 1), :], sems.at[1]).start()
        return carry

    lax.fori_loop(0, tm // ROW_UNROLL, issue, 0)

    def drain(grp, carry):
        for u in range(TOP_K * ROW_UNROLL):
            pltpu.make_async_copy(ys_hbm.at[pl.ds(0, 1), :], ybuf.at[pl.ds(0, 1), :],
                                  sems.at[1]).wait()
        return carry

    lax.fori_loop(0, tm // ROW_UNROLL, drain, 0)
    y = jnp.zeros(x_ref.shape, F32)
    for k in range(TOP_K):
        y = y + gate_ref[:, k:k + 1] * _unpack_rows(ybuf[k * tm:(k + 1) * tm, :])
    o_ref[...] = x_ref[...] + gt_ref[0] * _rms_norm(y, ng_ref[...])


def _combine(x2, gates_tk, norm_g, gt, dest_tiles, ys, seq, tm):
    t, d = x2.shape
    tpb = seq // tm
    return pl.pallas_call(
        _combine_kernel,
        grid=(t // tm,),
        in_specs=[pl.BlockSpec((tm, d), lambda i: (i, 0)),
                  pl.BlockSpec((tm, TOP_K), lambda i: (i, 0)),
                  pl.BlockSpec((1, d), lambda i: (0, 0)),
                  pl.BlockSpec((1, 1, d), lambda i: (i // tpb, 0, 0)),
                  pl.BlockSpec(memory_space=pl.ANY),
                  pl.BlockSpec(memory_space=pl.ANY)],
        out_specs=pl.BlockSpec((tm, d), lambda i: (i, 0)),
        out_shape=jax.ShapeDtypeStruct((t, d), F32),
        scratch_shapes=[pltpu.VMEM((TOP_K * tm, d // 2), U32),
                        pltpu.SMEM((2, TOP_K * tm), I32),
                        pltpu.SemaphoreType.DMA((2,))],
        compiler_params=_params(("arbitrary",), VMEM_LIMIT),
        name="moe_combine",
    )(x2, gates_tk, norm_g.reshape(1, d), gt, dest_tiles, ys)


def _moe(x2, norm_pre, sc, sh, norm_post, gt, router_w, router_b, layer, w1, b1_perm, w2, b2, seq,
         tm=512):
    t, d = x2.shape
    ne = N_EXPERTS
    bm = MOE_BLOCK
    top_idx, gates, rank, cnt = _router(x2, norm_pre, sc, sh, router_w.T, router_b, seq)
    counts = cnt[:, 0].astype(I32)
    padded = ((counts + bm - 1) // bm) * bm
    padded_end = jnp.cumsum(padded)
    padded_start = padded_end - padded
    start_of = jnp.sum(jnp.where(top_idx[..., None] == jnp.arange(ne), padded_start, 0), axis=-1)
    dest = start_of + rank
    n_slots = t * TOP_K + ne * bm
    n_blocks = n_slots // bm
    blk_row = jnp.arange(n_blocks, dtype=I32) * bm
    blk_expert = jnp.minimum(jnp.sum((padded_end[None, :] <= blk_row[:, None]).astype(I32), axis=1),
                             ne - 1)
    blk_first = jnp.concatenate([jnp.ones((1,), I32),
                                 (blk_expert[1:] != blk_expert[:-1]).astype(I32)])
    n_used = (padded_end[-1:] // bm).astype(I32)
    dest_tiles = dest.reshape(TOP_K, t // tm, tm).transpose(1, 0, 2).reshape(t // tm, TOP_K * tm)
    xs = _dispatch(x2, norm_pre, sc, sh, dest_tiles, jnp.zeros((n_slots, d // 2), U32), seq, tm)
    ys = _experts(xs, blk_expert, blk_first, n_used, layer, w1, b1_perm, w2, b2)
    return _combine(x2, gates.T, norm_post, gt, dest_tiles, ys, seq, tm)


def kernel(x, c, mod_w, mod_b, norm_mix_pre, norm_mix_post, norm_ffn_pre, norm_ffn_post, ab_w_in, ab_mu_prev, ab_mu_next, rwkv_w0, rwkv_w2, rwkv_a0, rwkv_a2, rwkv_k_k, rwkv_k_a, rwkv_r_k, rwkv_lnx_g, rwkv_lnx_b, rwkv_g2, conv_w, conv_b, conv_gn_g, conv_gn_b, ab_w_out, attn_w_qkv, attn_sink, attn_w_o, rel_bias, router_w, router_b, exp_w1, exp_b1, exp_w2, exp_b2):
    bsz, seq, d = x.shape
    depth = mod_w.shape[0]
    t = bsz * seq
    x2 = x.reshape(t, d)
    mod = _mod_all(c, mod_w, mod_b)
    bias = _attn_bias(rel_bias)
    ne, f2 = exp_b1.shape[1:]
    b1_perm = exp_b1.reshape(depth, ne, f2 // 256, 128, 2).swapaxes(-1, -2).reshape(depth, ne, f2)
    for layer in range(depth):
        j = layer // 2
        sh1, sc1, gt1, sh2, sc2, gt2 = [m.reshape(bsz, 1, d) for m in jnp.split(mod[layer], 6, axis=-1)]
        if layer % 2 == 0:
            pa, pb = _in_proj(x2, norm_mix_pre[layer], sc1, sh1, ab_w_in[j].astype(BF16),
                              (A_PROJ, 2 * CONV_WIDTH), seq)
            r, v, kk, kd, ad, lw, bonus, g = _rwkv_prep(
                pa, bsz, seq, ab_mu_prev[j], ab_mu_next[j], rwkv_w0[j], rwkv_w2[j], rwkv_a0[j],
                rwkv_a2[j], rwkv_k_k[j], rwkv_k_a[j], rwkv_r_k[j], rwkv_g2[j])
            yf, yb = _rwkv_scan(r, v, kk, kd, ad, lw, bsz, seq)
            cv = _conformer_conv(pb, bsz, seq, conv_w[j], conv_b[j], conv_gn_g[j], conv_gn_b[j])
            x2 = _mix_out(yf, yb, bonus, g, cv, x2, rwkv_lnx_g[j], rwkv_lnx_b[j],
                          ab_w_out[j].astype(BF16), norm_mix_post[layer], gt1, seq)
        else:
            (qkv,) = _in_proj(x2, norm_mix_pre[layer], sc1, sh1, attn_w_qkv[j].astype(BF16),
                              (Q_WIDTH + 2 * KV_WIDTH,), seq)
            y = _attention(qkv, bias, attn_sink[j], bsz, seq)
            x2 = _out_proj(y, x2, attn_w_o[j].astype(BF16), norm_mix_post[layer], gt1, seq)
        x2 = _moe(x2, norm_ffn_pre[layer], sc2, sh2, norm_ffn_post[layer], gt2, router_w[layer],
                  router_b[layer], layer, exp_w1, b1_perm, exp_w2, exp_b2, seq)
    return x2.reshape(bsz, seq, d)
```

```python
import functools

import jax
import jax.numpy as jnp
from jax import lax
from jax.experimental import pallas as pl
from jax.experimental.pallas import tpu as pltpu

F32 = jnp.float32
BF16 = jnp.bfloat16
I32 = jnp.int32
U32 = jnp.uint32
HIGHEST = lax.Precision.HIGHEST

D_MODEL = 1024
HEAD_DIM = 64
RWKV_WIDTH = 512
RWKV_HEADS = RWKV_WIDTH // HEAD_DIM
W_LORA = 64
A_LORA = 64
G_LORA = 128
A_PROJ = 3 * RWKV_WIDTH + W_LORA + A_LORA + G_LORA
CONV_WIDTH = D_MODEL - RWKV_WIDTH
CONV_KERNEL = 31
CONV_PAD = CONV_KERNEL // 2
CONV_GROUPS = 8
ATTN_HEADS = 16
ATTN_KV_HEADS = 2
ATTN_GROUP = ATTN_HEADS // ATTN_KV_HEADS
Q_WIDTH = ATTN_HEADS * HEAD_DIM
KV_WIDTH = ATTN_KV_HEADS * HEAD_DIM
ATTN_BLOCK = 128
WINDOW = 128
N_BUCKETS = 32
MAX_DISTANCE = 128
N_EXPERTS = 32
TOP_K = 4
SWIGLU_LIMIT = 7.0
SWIGLU_ALPHA = 1.702
RMS_EPS = 1e-6
GN_EPS = 1e-5
LNX_EPS = 64e-5
NEG = -1e30

CHUNK = 64
HALO = 16
MOE_BLOCK = 512
ROW_UNROLL = 8
VMEM_LIMIT = 56 * 1024 * 1024


def _params(sem, vmem=None):
    kw = dict(dimension_semantics=sem)
    if vmem is not None:
        kw["vmem_limit_bytes"] = vmem
    return pltpu.CompilerParams(**kw)


def _sigmoid(x):
    return 1.0 / (1.0 + jnp.exp(-x))


def _dot(a, b):
    return jnp.dot(a, b, preferred_element_type=F32)


def _dot_nt(a, b):
    return lax.dot_general(a, b, (((1,), (1,)), ((), ())), preferred_element_type=F32)


def _dot_exact_lhs(a_bf16, x):
    xh = x.astype(BF16)
    xl = (x - xh.astype(F32)).astype(BF16)
    return _dot(a_bf16, xh) + _dot(a_bf16, xl)


def _dot_exact_rhs(x, a_bf16):
    xh = x.astype(BF16)
    xl = (x - xh.astype(F32)).astype(BF16)
    return _dot(xh, a_bf16) + _dot(xl, a_bf16)


def _group_matrix(width, group, value):
    r = lax.broadcasted_iota(I32, (width, width), 0) // group
    c = lax.broadcasted_iota(I32, (width, width), 1) // group
    return jnp.where(r == c, value, 0.0).astype(BF16)


def _rms_norm(x, g):
    ms = jnp.mean(x * x, axis=-1, keepdims=True)
    return x * lax.rsqrt(ms + RMS_EPS) * g


def _norm_mod(x, g, sc, sh):
    return _rms_norm(x, g) * (1.0 + sc) + sh


def _mod_kernel(c_ref, w_ref, b_ref, o_ref):
    c = c_ref[...]
    ca = c * _sigmoid(c)
    o_ref[0] = jnp.dot(ca, w_ref[0], preferred_element_type=F32, precision=HIGHEST) + b_ref[0]


def _mod_all(c, mod_w, mod_b):
    depth, d, n = mod_w.shape
    bsz = c.shape[0]
    tn = 1536
    return pl.pallas_call(
        _mod_kernel,
        grid=(depth, n // tn),
        in_specs=[pl.BlockSpec((bsz, d), lambda l, j: (0, 0)),
                  pl.BlockSpec((1, d, tn), lambda l, j: (l, 0, j)),
                  pl.BlockSpec((1, 1, tn), lambda l, j: (l, 0, j))],
        out_specs=pl.BlockSpec((1, bsz, tn), lambda l, j: (l, 0, j)),
        out_shape=jax.ShapeDtypeStruct((depth, bsz, n), F32),
        compiler_params=_params(("parallel", "parallel"), VMEM_LIMIT),
        name="mod",
    )(c, mod_w, mod_b.reshape(depth, 1, n))


def _in_proj_kernel(x_ref, g_ref, sc_ref, sh_ref, w_ref, *o_refs):
    h = _norm_mod(x_ref[...], g_ref[...], sc_ref[0], sh_ref[0])
    p = _dot(h.astype(BF16), w_ref[...])
    off = 0
    for o_ref in o_refs:
        wdt = o_ref.shape[1]
        o_ref[...] = p[:, off:off + wdt]
        off += wdt


def _in_proj(x2, g, sc, sh, w_bf16, splits, seq, tm=256):
    t, d = x2.shape
    n = w_bf16.shape[1]
    tpb = seq // tm
    row = lambda i: (i, 0)
    per_b = lambda i: (i // tpb, 0, 0)
    return pl.pallas_call(
        _in_proj_kernel,
        grid=(t // tm,),
        in_specs=[pl.BlockSpec((tm, d), row),
                  pl.BlockSpec((1, d), lambda i: (0, 0)),
                  pl.BlockSpec((1, 1, d), per_b),
                  pl.BlockSpec((1, 1, d), per_b),
                  pl.BlockSpec((d, n), lambda i: (0, 0))],
        out_specs=[pl.BlockSpec((tm, s), row) for s in splits],
        out_shape=[jax.ShapeDtypeStruct((t, s), F32) for s in splits],
        compiler_params=_params(("parallel",), VMEM_LIMIT),
        name="in_proj",
    )(x2, g.reshape(1, d), sc, sh, w_bf16)


def _rwkv_prep_kernel(pm_ref, pp_ref, pn_ref, mup_ref, mun_ref, w0_ref, w2_ref, a0_ref, a2_ref,
                      kk_ref, ka_ref, rk_ref, g2_ref,
                      r_o, v_o, kk_o, kd_o, ad_o, lw_o, bonus_o, g_o):
    i = pl.program_id(1)
    n = pl.num_programs(1)
    p = pm_ref[...]
    ts = p.shape[0]
    rows = lax.broadcasted_iota(I32, (ts, 1), 0)
    prev_row = jnp.where(i > 0, pp_ref[HALO - 1:HALO, :], 0.0)
    next_row = jnp.where(i < n - 1, pn_ref[0:1, :], 0.0)
    prev = jnp.where(rows == 0, prev_row, pltpu.roll(p, 1, 0))
    nxt = jnp.where(rows == ts - 1, next_row, pltpu.roll(p, ts - 1, 0))
    pa = p + mup_ref[...] * (prev - p) + mun_ref[...] * (nxt - p)

    w = RWKV_WIDTH
    r = pa[:, 0:w]
    k = pa[:, w:2 * w]
    v = pa[:, 2 * w:3 * w]
    w_lo = pa[:, 3 * w:3 * w + W_LORA]
    a_lo = pa[:, 3 * w + W_LORA:3 * w + W_LORA + A_LORA]
    g_lo = pa[:, 3 * w + W_LORA + A_LORA:]

    ones = _group_matrix(w, HEAD_DIM, 1.0)
    kk = k * kk_ref[...]
    ss = _dot_exact_rhs(kk * kk, ones)
    kk = kk / jnp.maximum(jnp.sqrt(ss), 1e-12)
    tw = jnp.tanh(w_lo)
    r_o[...] = r
    v_o[...] = v
    kk_o[...] = kk
    bonus = jnp.zeros_like(r)
    for d in range(2):
        z = w0_ref[d] + jnp.dot(tw, w2_ref[d], preferred_element_type=F32, precision=HIGHEST)
        nz = -z
        softplus = jnp.maximum(nz, 0.0) + jnp.log(1.0 + jnp.exp(-jnp.abs(nz)))
        w_log = -softplus - 0.5
        lw_o[d] = -jnp.exp(w_log)
        a = _sigmoid(a0_ref[d] + jnp.dot(a_lo, a2_ref[d], preferred_element_type=F32,
                                         precision=HIGHEST))
        k_d = k * (1.0 + (a - 1.0) * ka_ref[...])
        ad_o[d] = a
        kd_o[d] = k_d
        bonus = bonus + _dot_exact_rhs(r * k_d * rk_ref[...], ones) * v
    bonus_o[...] = bonus
    g_o[...] = jnp.dot(_sigmoid(g_lo), g2_ref[...], preferred_element_type=F32, precision=HIGHEST)


def _rwkv_prep(pa, bsz, seq, mu_prev, mu_next, w0, w2, a0, a2, k_k, k_a, r_k, g2, ts=256):
    t = pa.shape[0]
    w = RWKV_WIDTH
    nt = seq // ts
    hb = ts // HALO
    main = lambda b, i: (b * nt + i, 0)
    prev = lambda b, i: (jnp.maximum((b * nt + i) * hb - 1, 0), 0)
    nxt = lambda b, i: (jnp.minimum((b * nt + i + 1) * hb, t // HALO - 1), 0)
    c0 = lambda b, i: (0, 0)
    c3 = lambda b, i: (0, 0, 0)
    dmain = lambda b, i: (0, b * nt + i, 0)
    tok = jax.ShapeDtypeStruct((t, w), F32)
    tok2 = jax.ShapeDtypeStruct((2, t, w), F32)
    return pl.pallas_call(
        _rwkv_prep_kernel,
        grid=(bsz, nt),
        in_specs=[pl.BlockSpec((ts, A_PROJ), main),
                  pl.BlockSpec((HALO, A_PROJ), prev),
                  pl.BlockSpec((HALO, A_PROJ), nxt),
                  pl.BlockSpec((1, A_PROJ), c0),
                  pl.BlockSpec((1, A_PROJ), c0),
                  pl.BlockSpec((2, 1, w), c3),
                  pl.BlockSpec((2, W_LORA, w), c3),
                  pl.BlockSpec((2, 1, w), c3),
                  pl.BlockSpec((2, A_LORA, w), c3),
                  pl.BlockSpec((1, w), c0),
                  pl.BlockSpec((1, w), c0),
                  pl.BlockSpec((1, w), c0),
                  pl.BlockSpec((G_LORA, w), c0)],
        out_specs=[pl.BlockSpec((ts, w), main)] * 3
                  + [pl.BlockSpec((2, ts, w), dmain)] * 3
                  + [pl.BlockSpec((ts, w), main)] * 2,
        out_shape=[tok, tok, tok, tok2, tok2, tok2, tok, tok],
        compiler_params=_params(("parallel", "parallel"), VMEM_LIMIT),
        name="rwkv_prep",
    )(pa, pa, pa, mu_prev.reshape(1, -1), mu_next.reshape(1, -1), w0.reshape(2, 1, w), w2,
      a0.reshape(2, 1, w), a2, k_k.reshape(1, w), k_a.reshape(1, w), r_k.reshape(1, w), g2)


def _wkv_scaled(r, kd, v, kk, a, lw, reverse):
    c = r.shape[0]
    ri = lax.broadcasted_iota(I32, (c, c), 0)
    ci = lax.broadcasted_iota(I32, (c, c), 1)
    if reverse:
        incl, strict, last = ci >= ri, ci > ri, 0
    else:
        incl, strict, last = ci <= ri, ci < ri, c - 1
    cum = _dot_exact_lhs(incl.astype(BF16), lw)
    cum_last = cum[last:last + 1, :]
    w_end = jnp.exp(cum_last - cum)
    w_inv = jnp.exp(-cum)
    b = kk * a
    return dict(rt=r * jnp.exp(cum), kt=kd * w_inv, bt=b * w_inv, at=-kk * jnp.exp(cum - lw),
                bk_t=jnp.concatenate([b * w_end, kd * w_end], axis=0).T,
                w_c=jnp.exp(cum_last), v=v, incl=incl, strict=strict)


def _wkv_chunks(ops, states):
    c = ops[0]["rt"].shape[0]
    hd = HEAD_DIM
    n_heads = ops[0]["rt"].shape[1] // hd
    streams = [(d, h) for d in range(len(ops)) for h in range(n_heads)]
    ri = lax.broadcasted_iota(I32, (c, c), 0)
    ci = lax.broadcasted_iota(I32, (c, c), 1)
    eye = (ri == ci).astype(F32)
    zeros = jnp.zeros((c, hd), F32)

    def head(d, h, name):
        return ops[d][name][:, h * hd:(h + 1) * hd]

    ll = [_dot_nt(jnp.concatenate([head(d, h, "at"), head(d, h, "rt")], axis=0),
                  jnp.concatenate([head(d, h, "bt"), head(d, h, "kt")], axis=0))
          for d, h in streams]
    l_ab = [jnp.where(ops[d]["strict"], m[:c, :c], 0.0) for (d, h), m in zip(streams, ll)]
    l_ak = [jnp.where(ops[d]["strict"], m[:c, c:], 0.0) for (d, h), m in zip(streams, ll)]
    l_rr = [jnp.concatenate([jnp.where(ops[d]["incl"], m[c:, :c], 0.0),
                             jnp.where(ops[d]["incl"], m[c:, c:], 0.0)], axis=1)
            for (d, h), m in zip(streams, ll)]
    tinv = [eye + m for m in l_ab]
    lp = l_ab
    for _ in range(max(1, (c - 1).bit_length() - 1)):
        lp = [_dot(m, m) for m in lp]
        tinv = [t + _dot(m, t) for m, t in zip(lp, tinv)]
    lakv = [_dot(m, head(d, h, "v")) for (d, h), m in zip(streams, l_ak)]
    x = [_dot(t, jnp.concatenate([head(d, h, "at"), u], axis=1))
         for (d, h), t, u in zip(streams, tinv, lakv)]
    o2 = [_dot(jnp.concatenate([lr, ops[d]["bk_t"][h * hd:(h + 1) * hd, :]], axis=0),
               jnp.concatenate([xx, jnp.concatenate([zeros, head(d, h, "v")], axis=1)], axis=0))
          for (d, h), lr, xx in zip(streams, l_rr, x)]
    o3 = [_dot(jnp.concatenate([head(d, h, "rt") + m[:c, :hd],
                                eye[:hd, :hd] * head(d, h, "w_c") + m[c:, :hd]], axis=0),
               states[d][:, h * hd:(h + 1) * hd])
          for (d, h), m in zip(streams, o2)]
    ys, sts = [], []
    for d in range(len(ops)):
        sel = [(m2, m3) for (dd, h), m2, m3 in zip(streams, o2, o3) if dd == d]
        ys.append(jnp.concatenate([m3[:c] + m2[:c, hd:] for m2, m3 in sel], axis=1))
        sts.append(jnp.concatenate([m3[c:] + m2[c:, hd:] for m2, m3 in sel], axis=1))
    return ys, sts


def _rwkv_scan_kernel(rf, vf, kkf, kdf, adf, lwf, rb, vb, kkb, kdb, adb, lwb, yf_o, yb_o, st_ref):
    @pl.when(pl.program_id(1) == 0)
    def _():
        st_ref[...] = jnp.zeros_like(st_ref)

    ops = [_wkv_scaled(rf[...], kdf[0], vf[...], kkf[...], adf[0], lwf[0], False),
           _wkv_scaled(rb[...], kdb[0], vb[...], kkb[...], adb[0], lwb[0], True)]
    ys, sts = _wkv_chunks(ops, [st_ref[0], st_ref[1]])
    yf_o[...] = ys[0]
    yb_o[...] = ys[1]
    st_ref[0] = sts[0]
    st_ref[1] = sts[1]


def _rwkv_scan(r, v, kk, kd, ad, lw, bsz, seq):
    t, w = r.shape
    nc = seq // CHUNK
    fwd = lambda b, i: (b * nc + i, 0)
    bwd = lambda b, i: (b * nc + nc - 1 - i, 0)
    fwd0 = lambda b, i: (0, b * nc + i, 0)
    bwd1 = lambda b, i: (1, b * nc + nc - 1 - i, 0)
    tok = pl.BlockSpec((CHUNK, w), fwd)
    tokb = pl.BlockSpec((CHUNK, w), bwd)
    dir0 = pl.BlockSpec((1, CHUNK, w), fwd0)
    dir1 = pl.BlockSpec((1, CHUNK, w), bwd1)
    return pl.pallas_call(
        _rwkv_scan_kernel,
        grid=(bsz, nc),
        in_specs=[tok, tok, tok, dir0, dir0, dir0, tokb, tokb, tokb, dir1, dir1, dir1],
        out_specs=[tok, tokb],
        out_shape=[jax.ShapeDtypeStruct((t, w), F32)] * 2,
        scratch_shapes=[pltpu.VMEM((2, HEAD_DIM, w), F32)],
        compiler_params=_params(("parallel", "arbitrary"), VMEM_LIMIT),
        name="rwkv_scan",
    )(r, v, kk, kd, ad, lw, r, v, kk, kd, ad, lw)


def _conv_kernel(pm_ref, pp_ref, pn_ref, cw_ref, cb_ref, g_ref, b_ref, o_ref, hbuf):
    i = pl.program_id(1)
    n = pl.num_programs(1)
    ts = pm_ref.shape[0]
    cwid = CONV_WIDTH

    def glu(p):
        return p[:, :cwid] * _sigmoid(p[:, cwid:])

    hbuf[0:HALO, :] = jnp.where(i > 0, glu(pp_ref[...]), 0.0)
    hbuf[HALO:HALO + ts, :] = glu(pm_ref[...])
    hbuf[HALO + ts:2 * HALO + ts, :] = jnp.where(i < n - 1, glu(pn_ref[...]), 0.0)
    acc = jnp.zeros((ts, cwid), F32)
    for k in range(CONV_KERNEL):
        off = HALO - CONV_PAD + k
        acc = acc + cw_ref[k:k + 1, :] * hbuf[off:off + ts, :]
    acc = acc + cb_ref[...]
    avg = _group_matrix(cwid, cwid // CONV_GROUPS, 1.0 / (cwid // CONV_GROUPS))
    mu = _dot_exact_rhs(acc, avg)
    xc = acc - mu
    var = _dot_exact_rhs(xc * xc, avg)
    y = xc * lax.rsqrt(var + GN_EPS) * g_ref[...] + b_ref[...]
    o_ref[...] = y * _sigmoid(y)


def _conformer_conv(pb, bsz, seq, conv_w, conv_b, gn_g, gn_b, ts=512):
    t = pb.shape[0]
    cwid = CONV_WIDTH
    nt = seq // ts
    hb = ts // HALO
    main = lambda b, i: (b * nt + i, 0)
    prev = lambda b, i: (jnp.maximum((b * nt + i) * hb - 1, 0), 0)
    nxt = lambda b, i: (jnp.minimum((b * nt + i + 1) * hb, t // HALO - 1), 0)
    c0 = lambda b, i: (0, 0)
    return pl.pallas_call(
        _conv_kernel,
        grid=(bsz, nt),
        in_specs=[pl.BlockSpec((ts, 2 * cwid), main),
                  pl.BlockSpec((HALO, 2 * cwid), prev),
                  pl.BlockSpec((HALO, 2 * cwid), nxt),
                  pl.BlockSpec((CONV_KERNEL, cwid), c0),
                  pl.BlockSpec((1, cwid), c0),
                  pl.BlockSpec((1, cwid), c0),
                  pl.BlockSpec((1, cwid), c0)],
        out_specs=pl.BlockSpec((ts, cwid), main),
        out_shape=jax.ShapeDtypeStruct((t, cwid), F32),
        scratch_shapes=[pltpu.VMEM((ts + 2 * HALO, cwid), F32)],
        compiler_params=_params(("parallel", "parallel"), VMEM_LIMIT),
        name="conformer_conv",
    )(pb, pb, pb, conv_w, conv_b.reshape(1, cwid), gn_g.reshape(1, cwid), gn_b.reshape(1, cwid))


def _mix_out_kernel(yf_ref, yb_ref, bonus_ref, g_ref, cv_ref, x_ref, lg_ref, lb_ref, w_ref,
                    ng_ref, gt_ref, o_ref):
    w = RWKV_WIDTH
    wkv = yf_ref[...] + yb_ref[...]
    avg = _group_matrix(w, HEAD_DIM, 1.0 / HEAD_DIM)
    mu = _dot_exact_rhs(wkv, avg)
    xc = wkv - mu
    var = _dot_exact_rhs(xc * xc, avg)
    y = xc * lax.rsqrt(var + LNX_EPS) * lg_ref[...] + lb_ref[...] + bonus_ref[...]
    ya = y * g_ref[...]
    cat = jnp.concatenate([ya.astype(BF16), cv_ref[...].astype(BF16)], axis=1)
    z = _dot(cat, w_ref[...])
    o_ref[...] = x_ref[...] + gt_ref[0] * _rms_norm(z, ng_ref[...])


def _mix_out(yf, yb, bonus, g, cv, x2, lnx_g, lnx_b, w_out_bf16, norm_g, gt, seq, tm=256):
    t, d = x2.shape
    w = RWKV_WIDTH
    tpb = seq // tm
    row = lambda i: (i, 0)
    c0 = lambda i: (0, 0)
    half = pl.BlockSpec((tm, w), row)
    return pl.pallas_call(
        _mix_out_kernel,
        grid=(t // tm,),
        in_specs=[half, half, half, half, half,
                  pl.BlockSpec((tm, d), row),
                  pl.BlockSpec((1, w), c0),
                  pl.BlockSpec((1, w), c0),
                  pl.BlockSpec((d, d), c0),
                  pl.BlockSpec((1, d), c0),
                  pl.BlockSpec((1, 1, d), lambda i: (i // tpb, 0, 0))],
        out_specs=pl.BlockSpec((tm, d), row),
        out_shape=jax.ShapeDtypeStruct((t, d), F32),
        compiler_params=_params(("parallel",), VMEM_LIMIT),
        name="mix_out",
    )(yf, yb, bonus, g, cv, x2, lnx_g.reshape(1, w), lnx_b.reshape(1, w), w_out_bf16,
      norm_g.reshape(1, d), gt)


def _attn_kernel(sink_ref, q_ref, kp_ref, kc_ref, kn_ref, vp_ref, vc_ref, vn_ref, bias_ref, o_ref):
    i = pl.program_id(1)
    n = pl.num_programs(1)
    blk = ATTN_BLOCK
    hd = HEAD_DIM
    k = jnp.concatenate([kp_ref[...], kc_ref[...], kn_ref[...]], axis=0).astype(BF16)
    v = jnp.concatenate([vp_ref[...], vc_ref[...], vn_ref[...]], axis=0).astype(BF16)
    col = lax.broadcasted_iota(I32, (1, 3 * blk), 1)
    valid = ((col >= blk) | (i > 0)) & ((col < 2 * blk) | (i < n - 1))
    scale = hd ** -0.5
    kvs = range(ATTN_KV_HEADS)
    qs = [jnp.concatenate([q_ref[:, (kvh * ATTN_GROUP + g) * hd:(kvh * ATTN_GROUP + g + 1) * hd]
                           for g in range(ATTN_GROUP)], axis=0).astype(BF16) for kvh in kvs]
    scs = [_dot_nt(qs[kvh], k[:, kvh * hd:(kvh + 1) * hd]) for kvh in kvs]
    ps, denoms = [], []
    for kvh in kvs:
        sc = jnp.where(valid, scs[kvh] * scale + bias_ref[kvh], NEG)
        sink = sink_ref[kvh]
        m = jnp.maximum(jnp.max(sc, axis=-1, keepdims=True), sink)
        p = jnp.exp(sc - m)
        denoms.append(jnp.sum(p, axis=-1, keepdims=True) + jnp.exp(sink - m))
        ps.append(p.astype(BF16))
    for kvh in kvs:
        o = _dot(ps[kvh], v[:, kvh * hd:(kvh + 1) * hd]) / denoms[kvh]
        for g in range(ATTN_GROUP):
            h = kvh * ATTN_GROUP + g
            o_ref[:, h * hd:(h + 1) * hd] = o[g * blk:(g + 1) * blk, :]


def _attention(qkv, bias, sink, bsz, seq):
    t = qkv.shape[0]
    rows = ATTN_GROUP * ATTN_BLOCK
    bias = bias.reshape(ATTN_KV_HEADS, rows, 3 * ATTN_BLOCK)
    sink = jnp.repeat(sink.astype(F32), ATTN_BLOCK).reshape(ATTN_KV_HEADS, rows, 1)
    blk = ATTN_BLOCK
    nb = seq // blk
    kcol = Q_WIDTH // KV_WIDTH
    vcol = kcol + 1
    cur = lambda b, i: b * nb + i
    prv = lambda b, i: b * nb + jnp.maximum(i - 1, 0)
    nxt = lambda b, i: b * nb + jnp.minimum(i + 1, nb - 1)
    kv = lambda rowf, c: pl.BlockSpec((blk, KV_WIDTH), lambda b, i: (rowf(b, i), c))
    return pl.pallas_call(
        _attn_kernel,
        grid=(bsz, nb),
        in_specs=[pl.BlockSpec((ATTN_KV_HEADS, rows, 1), lambda b, i: (0, 0, 0)),
                  pl.BlockSpec((blk, Q_WIDTH), lambda b, i: (cur(b, i), 0)),
                  kv(prv, kcol), kv(cur, kcol), kv(nxt, kcol),
                  kv(prv, vcol), kv(cur, vcol), kv(nxt, vcol),
                  pl.BlockSpec((ATTN_KV_HEADS, rows, 3 * blk), lambda b, i: (0, 0, 0))],
        out_specs=pl.BlockSpec((blk, Q_WIDTH), lambda b, i: (cur(b, i), 0)),
        out_shape=jax.ShapeDtypeStruct((t, Q_WIDTH), F32),
        compiler_params=_params(("parallel", "parallel"), VMEM_LIMIT),
        name="attention",
    )(sink, qkv, qkv, qkv, qkv, qkv, qkv, qkv, bias)


def _t5_bucket(rel):
    n = -rel
    half = N_BUCKETS // 2
    ret = jnp.where(n < 0, half, 0)
    n = jnp.abs(n)
    max_exact = half // 2
    nf = jnp.maximum(n, 1).astype(F32)
    large = max_exact + (jnp.log(nf / max_exact) / jnp.log(MAX_DISTANCE / max_exact)
                         * (half - max_exact)).astype(I32)
    large = jnp.minimum(large, half - 1)
    return ret + jnp.where(n < max_exact, n, large)


def _attn_bias(rel_bias):
    blk = ATTN_BLOCK
    rel = jnp.arange(3 * blk)[None, :] - blk - jnp.arange(blk)[:, None]
    bias = jnp.transpose(rel_bias[_t5_bucket(rel)].astype(F32), (2, 0, 1))
    return jnp.where(jnp.abs(rel) <= WINDOW, bias, NEG)


def _out_proj_kernel(y_ref, x_ref, w_ref, ng_ref, gt_ref, o_ref):
    z = _dot(y_ref[...].astype(BF16), w_ref[...])
    o_ref[...] = x_ref[...] + gt_ref[0] * _rms_norm(z, ng_ref[...])


def _out_proj(y, x2, w_bf16, norm_g, gt, seq, tm=256):
    t, d = x2.shape
    tpb = seq // tm
    row = lambda i: (i, 0)
    c0 = lambda i: (0, 0)
    return pl.pallas_call(
        _out_proj_kernel,
        grid=(t // tm,),
        in_specs=[pl.BlockSpec((tm, d), row),
                  pl.BlockSpec((tm, d), row),
                  pl.BlockSpec((d, d), c0),
                  pl.BlockSpec((1, d), c0),
                  pl.BlockSpec((1, 1, d), lambda i: (i // tpb, 0, 0))],
        out_specs=pl.BlockSpec((tm, d), row),
        out_shape=jax.ShapeDtypeStruct((t, d), F32),
        compiler_params=_params(("parallel",), VMEM_LIMIT),
        name="out_proj",
    )(y, x2, w_bf16, norm_g.reshape(1, d), gt)


def _router_kernel(x_ref, g_ref, sc_ref, sh_ref, rw_ref, rb_ref, idx_o, gate_o, rank_o, cnt_o,
                   cnt_ref):
    @pl.when(pl.program_id(0) == 0)
    def _():
        cnt_ref[...] = jnp.zeros_like(cnt_ref)

    h = _norm_mod(x_ref[...], g_ref[...], sc_ref[0], sh_ref[0])
    tm = h.shape[0]
    ne = N_EXPERTS
    logits = lax.dot_general(rw_ref[...], h, (((1,), (1,)), ((), ())),
                             preferred_element_type=F32, precision=HIGHEST) + rb_ref[...]
    eidx = lax.broadcasted_iota(I32, (ne, tm), 0)
    vals, sels = [], []
    for k in range(TOP_K):
        m = jnp.max(logits, axis=0, keepdims=True)
        idx = jnp.min(jnp.where(logits == m, eidx, ne), axis=0, keepdims=True)
        sel = eidx == idx
        vals.append(m)
        sels.append(sel)
        idx_o[k:k + 1, :] = idx
        logits = jnp.where(sel, -jnp.inf, logits)
    es = [jnp.exp(vk - vals[0]) for vk in vals]
    tot = es[0] + es[1] + es[2] + es[3]
    for k in range(TOP_K):
        gate_o[k:k + 1, :] = es[k] / tot
    onehot = (sels[0] | sels[1] | sels[2] | sels[3])
    before = (lax.broadcasted_iota(I32, (tm, tm), 0) < lax.broadcasted_iota(I32, (tm, tm), 1))
    cum = _dot(onehot.astype(BF16), before.astype(BF16)) + cnt_ref[:, 0:1]
    for k in range(TOP_K):
        rank_o[k:k + 1, :] = jnp.sum(jnp.where(sels[k], cum, 0.0), axis=0,
                                     keepdims=True).astype(I32)
    cnt_ref[...] = cnt_ref[...] + jnp.sum(onehot.astype(F32), axis=1, keepdims=True)
    cnt_o[...] = cnt_ref[...]


def _router(x2, g, sc, sh, router_w_t, router_b, seq, tm=512):
    t, d = x2.shape
    ne = N_EXPERTS
    tpb = seq // tm
    per_b = lambda i: (i // tpb, 0, 0)
    c0 = lambda i: (0, 0)
    col = lambda i: (0, i)
    return pl.pallas_call(
        _router_kernel,
        grid=(t // tm,),
        in_specs=[pl.BlockSpec((tm, d), lambda i: (i, 0)),
                  pl.BlockSpec((1, d), c0),
                  pl.BlockSpec((1, 1, d), per_b),
                  pl.BlockSpec((1, 1, d), per_b),
                  pl.BlockSpec((ne, d), c0),
                  pl.BlockSpec((ne, 1), c0)],
        out_specs=[pl.BlockSpec((TOP_K, tm), col),
                   pl.BlockSpec((TOP_K, tm), col),
                   pl.BlockSpec((TOP_K, tm), col),
                   pl.BlockSpec((ne, 128), c0)],
        out_shape=[jax.ShapeDtypeStruct((TOP_K, t), I32),
                   jax.ShapeDtypeStruct((TOP_K, t), F32),
                   jax.ShapeDtypeStruct((TOP_K, t), I32),
                   jax.ShapeDtypeStruct((ne, 128), F32)],
        scratch_shapes=[pltpu.VMEM((ne, 128), F32)],
        compiler_params=_params(("arbitrary",), VMEM_LIMIT),
        name="router",
    )(x2, g.reshape(1, d), sc, sh, router_w_t, router_b.reshape(ne, 1))


def _pack_rows(h):
    half = h.shape[1] // 2
    hi = lax.bitcast_convert_type(h[:, :half].astype(BF16).astype(F32), U32)
    lo = lax.bitcast_convert_type(h[:, half:].astype(BF16).astype(F32), U32)
    return (hi & jnp.uint32(0xFFFF0000)) | (lo >> 16)


def _unpack_rows(p):
    hi = lax.bitcast_convert_type(p & jnp.uint32(0xFFFF0000), F32)
    lo = lax.bitcast_convert_type(p << 16, F32)
    return jnp.concatenate([hi, lo], axis=1)


def _tile_indices(dest_hbm, dest_smem, sem):
    i = pl.program_id(0)
    n = pl.num_programs(0)
    slot = i % 2

    def fetch(step, s):
        return pltpu.make_async_copy(dest_hbm.at[step], dest_smem.at[s], sem)

    @pl.when(i == 0)
    def _():
        fetch(0, 0).start()

    fetch(i, slot).wait()

    @pl.when(i + 1 < n)
    def _():
        fetch(i + 1, 1 - slot).start()

    return slot


def _dispatch_kernel(x_ref, g_ref, sc_ref, sh_ref, dest_hbm, xs_in, xs_hbm, hbuf, dest_smem, sems):
    del xs_in
    tm = x_ref.shape[0]
    hbuf[...] = _pack_rows(_norm_mod(x_ref[...], g_ref[...], sc_ref[0], sh_ref[0]))
    slot = _tile_indices(dest_hbm, dest_smem, sems.at[0])

    def issue(grp, carry):
        for u in range(ROW_UNROLL):
            row = grp * ROW_UNROLL + u
            for k in range(TOP_K):
                pltpu.make_async_copy(hbuf.at[pl.ds(row, 1), :],
                                      xs_hbm.at[pl.ds(dest_smem[slot, k * tm + row], 1), :],
                                      sems.at[1]).start()
        return carry

    lax.fori_loop(0, tm // ROW_UNROLL, issue, 0)

    for k in range(TOP_K):
        pltpu.make_async_copy(hbuf, xs_hbm.at[pl.ds(0, tm), :], sems.at[1]).wait()


def _dispatch(x2, g, sc, sh, dest_tiles, xs_init, seq, tm):
    t, d = x2.shape
    tpb = seq // tm
    per_b = lambda i: (i // tpb, 0, 0)
    return pl.pallas_call(
        _dispatch_kernel,
        grid=(t // tm,),
        in_specs=[pl.BlockSpec((tm, d), lambda i: (i, 0)),
                  pl.BlockSpec((1, d), lambda i: (0, 0)),
                  pl.BlockSpec((1, 1, d), per_b),
                  pl.BlockSpec((1, 1, d), per_b),
                  pl.BlockSpec(memory_space=pl.ANY),
                  pl.BlockSpec(memory_space=pl.ANY)],
        out_specs=pl.BlockSpec(memory_space=pl.ANY),
        out_shape=jax.ShapeDtypeStruct(xs_init.shape, U32),
        scratch_shapes=[pltpu.VMEM((tm, d // 2), U32),
                        pltpu.SMEM((2, TOP_K * tm), I32),
                        pltpu.SemaphoreType.DMA((2,))],
        input_output_aliases={5: 0},
        compiler_params=_params(("arbitrary",), VMEM_LIMIT),
        name="moe_dispatch",
    )(x2, g.reshape(1, d), sc, sh, dest_tiles, xs_init)


def _expert_kernel(be_ref, first_ref, used_ref, xs_ref, w1_ref, b1_ref, w2_ref, b2_ref, ys_ref,
                   w1p, w2p):
    i = pl.program_id(0)
    f2 = w1_ref.shape[3]
    pw = 256

    @pl.when((first_ref[i] == 1) & (i < used_ref[0]))
    def _():
        r = lax.broadcasted_iota(I32, (pw, pw), 0)
        c = lax.broadcasted_iota(I32, (pw, pw), 1)
        src = jnp.where(c < pw // 2, 2 * c, 2 * (c - pw // 2) + 1)
        perm = (r == src).astype(BF16)
        for j in range(f2 // pw):
            w1p[:, j * pw:(j + 1) * pw] = _dot(w1_ref[0, 0, :, j * pw:(j + 1) * pw].astype(BF16),
                                               perm).astype(BF16)
        w2p[...] = w2_ref[0, 0].astype(BF16)

    @pl.when(i < used_ref[0])
    def _():
        x = _unpack_rows(xs_ref[...]).astype(BF16)
        hcat = _dot(x, w1p[...]) + b1_ref[0, 0]
        acts = []
        for j in range(f2 // pw):
            x_glu = jnp.minimum(hcat[:, j * pw:j * pw + pw // 2], SWIGLU_LIMIT)
            x_lin = jnp.clip(hcat[:, j * pw + pw // 2:(j + 1) * pw], -SWIGLU_LIMIT, SWIGLU_LIMIT)
            acts.append(x_glu * _sigmoid(SWIGLU_ALPHA * x_glu) * (x_lin + 1.0))
        act = jnp.concatenate(acts, axis=1).astype(BF16)
        ys_ref[...] = _pack_rows(_dot(act, w2p[...]) + b2_ref[0, 0])

    @pl.when(i >= used_ref[0])
    def _():
        ys_ref[...] = jnp.zeros_like(ys_ref)


def _experts(xs, blk_expert, blk_first, n_used, layer, w1, b1_perm, w2, b2):
    n_slots, dp = xs.shape
    _, ne, d, f2 = w1.shape
    f = w2.shape[2]
    bm = MOE_BLOCK
    wmap = lambda i, be, fi, us: (layer, be[i], 0, 0)
    grid_spec = pltpu.PrefetchScalarGridSpec(
        num_scalar_prefetch=3,
        grid=(n_slots // bm,),
        in_specs=[pl.BlockSpec((bm, dp), lambda i, be, fi, us: (i, 0)),
                  pl.BlockSpec((1, 1, d, f2), wmap),
                  pl.BlockSpec((1, 1, 1, f2), wmap),
                  pl.BlockSpec((1, 1, f, d), wmap),
                  pl.BlockSpec((1, 1, 1, d), wmap)],
        out_specs=pl.BlockSpec((bm, dp), lambda i, be, fi, us: (i, 0)),
        scratch_shapes=[pltpu.VMEM((d, f2), BF16), pltpu.VMEM((f, d), BF16)],
    )
    depth = w1.shape[0]
    return pl.pallas_call(
        _expert_kernel,
        grid_spec=grid_spec,
        out_shape=jax.ShapeDtypeStruct((n_slots, dp), U32),
        compiler_params=_params(("arbitrary",), VMEM_LIMIT),
        name="moe_experts",
    )(blk_expert, blk_first, n_used, xs, w1, b1_perm.reshape(depth, ne, 1, f2), w2,
      b2.reshape(depth, ne, 1, d))


def _combine_kernel(x_ref, gate_ref, ng_ref, gt_ref, dest_hbm, ys_hbm, o_ref, ybuf, dest_smem, sems):
    tm = x_ref.shape[0]
    slot = _tile_indices(dest_hbm, dest_smem, sems.at[0])

    def issue(grp, carry):
        for u in range(TOP_K * ROW_UNROLL):
            j = grp * (TOP_K * ROW_UNROLL) + u
            pltpu.make_async_copy(ys_hbm.at[pl.ds(dest_smem[slot, j], 1), :],
                                  ybuf.at[pl.ds(j, 1), :], sems.at[1]).start()
        return carry

    lax.fori_loop(0, tm // ROW_UNROLL, issue, 0)
    pltpu.make_async_copy(ys_hbm.at[pl.ds(0, TOP_K * tm), :], ybuf, sems.at[1]).wait()
    y = jnp.zeros(x_ref.shape, F32)
    for k in range(TOP_K):
        y = y + gate_ref[:, k:k + 1] * _unpack_rows(ybuf[k * tm:(k + 1) * tm, :])
    o_ref[...] = x_ref[...] + gt_ref[0] * _rms_norm(y, ng_ref[...])


def _combine(x2, gates_tk, norm_g, gt, dest_tiles, ys, seq, tm):
    t, d = x2.shape
    tpb = seq // tm
    return pl.pallas_call(
        _combine_kernel,
        grid=(t // tm,),
        in_specs=[pl.BlockSpec((tm, d), lambda i: (i, 0)),
                  pl.BlockSpec((tm, TOP_K), lambda i: (i, 0)),
                  pl.BlockSpec((1, d), lambda i: (0, 0)),
                  pl.BlockSpec((1, 1, d), lambda i: (i // tpb, 0, 0)),
                  pl.BlockSpec(memory_space=pl.ANY),
                  pl.BlockSpec(memory_space=pl.ANY)],
        out_specs=pl.BlockSpec((tm, d), lambda i: (i, 0)),
        out_shape=jax.ShapeDtypeStruct((t, d), F32),
        scratch_shapes=[pltpu.VMEM((TOP_K * tm, d // 2), U32),
                        pltpu.SMEM((2, TOP_K * tm), I32),
                        pltpu.SemaphoreType.DMA((2,))],
        compiler_params=_params(("arbitrary",), VMEM_LIMIT),
        name="moe_combine",
    )(x2, gates_tk, norm_g.reshape(1, d), gt, dest_tiles, ys)


def _moe(x2, norm_pre, sc, sh, norm_post, gt, router_w, router_b, layer, w1, b1_perm, w2, b2, seq,
         tm=512):
    t, d = x2.shape
    ne = N_EXPERTS
    bm = MOE_BLOCK
    top_idx, gates, rank, cnt = _router(x2, norm_pre, sc, sh, router_w.T, router_b, seq)
    counts = cnt[:, 0].astype(I32)
    padded = ((counts + bm - 1) // bm) * bm
    padded_end = jnp.cumsum(padded)
    padded_start = padded_end - padded
    start_of = jnp.sum(jnp.where(top_idx[..., None] == jnp.arange(ne), padded_start, 0), axis=-1)
    dest = start_of + rank
    n_slots = t * TOP_K + ne * bm
    n_blocks = n_slots // bm
    blk_row = jnp.arange(n_blocks, dtype=I32) * bm
    blk_expert = jnp.minimum(jnp.sum((padded_end[None, :] <= blk_row[:, None]).astype(I32), axis=1),
                             ne - 1)
    blk_first = jnp.concatenate([jnp.ones((1,), I32),
                                 (blk_expert[1:] != blk_expert[:-1]).astype(I32)])
    n_used = (padded_end[-1:] // bm).astype(I32)
    dest_tiles = dest.reshape(TOP_K, t // tm, tm).transpose(1, 0, 2).reshape(t // tm, TOP_K * tm)
    xs = _dispatch(x2, norm_pre, sc, sh, dest_tiles, jnp.zeros((n_slots, d // 2), U32), seq, tm)
    ys = _experts(xs, blk_expert, blk_first, n_used, layer, w1, b1_perm, w2, b2)
    return _combine(x2, gates.T, norm_post, gt, dest_tiles, ys, seq, tm)


def kernel(x, c, mod_w, mod_b, norm_mix_pre, norm_mix_post, norm_ffn_pre, norm_ffn_post, ab_w_in, ab_mu_prev, ab_mu_next, rwkv_w0, rwkv_w2, rwkv_a0, rwkv_a2, rwkv_k_k, rwkv_k_a, rwkv_r_k, rwkv_lnx_g, rwkv_lnx_b, rwkv_g2, conv_w, conv_b, conv_gn_g, conv_gn_b, ab_w_out, attn_w_qkv, attn_sink, attn_w_o, rel_bias, router_w, router_b, exp_w1, exp_b1, exp_w2, exp_b2):
    bsz, seq, d = x.shape
    depth = mod_w.shape[0]
    t = bsz * seq
    x2 = x.reshape(t, d)
    mod = _mod_all(c, mod_w, mod_b)
    bias = _attn_bias(rel_bias)
    ne, f2 = exp_b1.shape[1:]
    b1_perm = exp_b1.reshape(depth, ne, f2 // 256, 128, 2).swapaxes(-1, -2).reshape(depth, ne, f2)
    for layer in range(depth):
        j = layer // 2
        sh1, sc1, gt1, sh2, sc2, gt2 = [m.reshape(bsz, 1, d) for m in jnp.split(mod[layer], 6, axis=-1)]
        if layer % 2 == 0:
            pa, pb = _in_proj(x2, norm_mix_pre[layer], sc1, sh1, ab_w_in[j].astype(BF16),
                              (A_PROJ, 2 * CONV_WIDTH), seq)
            r, v, kk, kd, ad, lw, bonus, g = _rwkv_prep(
                pa, bsz, seq, ab_mu_prev[j], ab_mu_next[j], rwkv_w0[j], rwkv_w2[j], rwkv_a0[j],
                rwkv_a2[j], rwkv_k_k[j], rwkv_k_a[j], rwkv_r_k[j], rwkv_g2[j])
            yf, yb = _rwkv_scan(r, v, kk, kd, ad, lw, bsz, seq)
            cv = _conformer_conv(pb, bsz, seq, conv_w[j], conv_b[j], conv_gn_g[j], conv_gn_b[j])
            x2 = _mix_out(yf, yb, bonus, g, cv, x2, rwkv_lnx_g[j], rwkv_lnx_b[j],
                          ab_w_out[j].astype(BF16), norm_mix_post[layer], gt1, seq)
        else:
            (qkv,) = _in_proj(x2, norm_mix_pre[layer], sc1, sh1, attn_w_qkv[j].astype(BF16),
                              (Q_WIDTH + 2 * KV_WIDTH,), seq)
            y = _attention(qkv, bias, attn_sink[j], bsz, seq)
            x2 = _out_proj(y, x2, attn_w_o[j].astype(BF16), norm_mix_post[layer], gt1, seq)
        x2 = _moe(x2, norm_ffn_pre[layer], sc2, sh2, norm_ffn_post[layer], gt2, router_w[layer],
                  router_b[layer], layer, exp_w1, b1_perm, exp_w2, exp_b2, seq)
    return x2.reshape(bsz, seq, d)
```

```python
import functools

import jax
import jax.numpy as jnp
from jax import lax
from jax.experimental import pallas as pl
from jax.experimental.pallas import tpu as pltpu

F32 = jnp.float32
BF16 = jnp.bfloat16
I32 = jnp.int32
U32 = jnp.uint32
HIGHEST = lax.Precision.HIGHEST

D_MODEL = 1024
HEAD_DIM = 64
RWKV_WIDTH = 512
RWKV_HEADS = RWKV_WIDTH // HEAD_DIM
W_LORA = 64
A_LORA = 64
G_LORA = 128
A_PROJ = 3 * RWKV_WIDTH + W_LORA + A_LORA + G_LORA
CONV_WIDTH = D_MODEL - RWKV_WIDTH
CONV_KERNEL = 31
CONV_PAD = CONV_KERNEL // 2
CONV_GROUPS = 8
ATTN_HEADS = 16
ATTN_KV_HEADS = 2
ATTN_GROUP = ATTN_HEADS // ATTN_KV_HEADS
Q_WIDTH = ATTN_HEADS * HEAD_DIM
KV_WIDTH = ATTN_KV_HEADS * HEAD_DIM
ATTN_BLOCK = 128
WINDOW = 128
N_BUCKETS = 32
MAX_DISTANCE = 128
N_EXPERTS = 32
TOP_K = 4
SWIGLU_LIMIT = 7.0
SWIGLU_ALPHA = 1.702
RMS_EPS = 1e-6
GN_EPS = 1e-5
LNX_EPS = 64e-5
NEG = -1e30

CHUNK = 64
HALO = 16
MOE_BLOCK = 512
ROW_UNROLL = 8
VMEM_LIMIT = 56 * 1024 * 1024


def _params(sem, vmem=None):
    kw = dict(dimension_semantics=sem)
    if vmem is not None:
        kw["vmem_limit_bytes"] = vmem
    return pltpu.CompilerParams(**kw)


def _sigmoid(x):
    return 1.0 / (1.0 + jnp.exp(-x))


def _dot(a, b):
    return jnp.dot(a, b, preferred_element_type=F32)


def _dot_nt(a, b):
    return lax.dot_general(a, b, (((1,), (1,)), ((), ())), preferred_element_type=F32)


def _dot_exact_lhs(a_bf16, x):
    xh = x.astype(BF16)
    xl = (x - xh.astype(F32)).astype(BF16)
    return _dot(a_bf16, xh) + _dot(a_bf16, xl)


def _dot_exact_rhs(x, a_bf16):
    xh = x.astype(BF16)
    xl = (x - xh.astype(F32)).astype(BF16)
    return _dot(xh, a_bf16) + _dot(xl, a_bf16)


def _group_matrix(width, group, value):
    r = lax.broadcasted_iota(I32, (width, width), 0) // group
    c = lax.broadcasted_iota(I32, (width, width), 1) // group
    return jnp.where(r == c, value, 0.0).astype(BF16)


def _rms_norm(x, g):
    ms = jnp.mean(x * x, axis=-1, keepdims=True)
    return x * lax.rsqrt(ms + RMS_EPS) * g


def _norm_mod(x, g, sc, sh):
    return _rms_norm(x, g) * (1.0 + sc) + sh


def _mod_kernel(c_ref, w_ref, b_ref, o_ref):
    c = c_ref[...]
    ca = c * _sigmoid(c)
    o_ref[0] = jnp.dot(ca, w_ref[0], preferred_element_type=F32, precision=HIGHEST) + b_ref[0]


def _mod_all(c, mod_w, mod_b):
    depth, d, n = mod_w.shape
    bsz = c.shape[0]
    tn = 1536
    return pl.pallas_call(
        _mod_kernel,
        grid=(depth, n // tn),
        in_specs=[pl.BlockSpec((bsz, d), lambda l, j: (0, 0)),
                  pl.BlockSpec((1, d, tn), lambda l, j: (l, 0, j)),
                  pl.BlockSpec((1, 1, tn), lambda l, j: (l, 0, j))],
        out_specs=pl.BlockSpec((1, bsz, tn), lambda l, j: (l, 0, j)),
        out_shape=jax.ShapeDtypeStruct((depth, bsz, n), F32),
        compiler_params=_params(("parallel", "parallel"), VMEM_LIMIT),
        name="mod",
    )(c, mod_w, mod_b.reshape(depth, 1, n))


def _in_proj_kernel(x_ref, g_ref, sc_ref, sh_ref, w_ref, *o_refs):
    h = _norm_mod(x_ref[...], g_ref[...], sc_ref[0], sh_ref[0])
    p = _dot(h.astype(BF16), w_ref[...])
    off = 0
    for o_ref in o_refs:
        wdt = o_ref.shape[1]
        o_ref[...] = p[:, off:off + wdt]
        off += wdt


def _in_proj(x2, g, sc, sh, w_bf16, splits, seq, tm=256):
    t, d = x2.shape
    n = w_bf16.shape[1]
    tpb = seq // tm
    row = lambda i: (i, 0)
    per_b = lambda i: (i // tpb, 0, 0)
    return pl.pallas_call(
        _in_proj_kernel,
        grid=(t // tm,),
        in_specs=[pl.BlockSpec((tm, d), row),
                  pl.BlockSpec((1, d), lambda i: (0, 0)),
                  pl.BlockSpec((1, 1, d), per_b),
                  pl.BlockSpec((1, 1, d), per_b),
                  pl.BlockSpec((d, n), lambda i: (0, 0))],
        out_specs=[pl.BlockSpec((tm, s), row) for s in splits],
        out_shape=[jax.ShapeDtypeStruct((t, s), F32) for s in splits],
        compiler_params=_params(("parallel",), VMEM_LIMIT),
        name="in_proj",
    )(x2, g.reshape(1, d), sc, sh, w_bf16)


def _rwkv_prep_kernel(pm_ref, pp_ref, pn_ref, mup_ref, mun_ref, w0_ref, w2_ref, a0_ref, a2_ref,
                      kk_ref, ka_ref, rk_ref, g2_ref,
                      r_o, v_o, kk_o, kd_o, ad_o, lw_o, bonus_o, g_o):
    i = pl.program_id(1)
    n = pl.num_programs(1)
    p = pm_ref[...]
    ts = p.shape[0]
    rows = lax.broadcasted_iota(I32, (ts, 1), 0)
    prev_row = jnp.where(i > 0, pp_ref[HALO - 1:HALO, :], 0.0)
    next_row = jnp.where(i < n - 1, pn_ref[0:1, :], 0.0)
    prev = jnp.where(rows == 0, prev_row, pltpu.roll(p, 1, 0))
    nxt = jnp.where(rows == ts - 1, next_row, pltpu.roll(p, ts - 1, 0))
    pa = p + mup_ref[...] * (prev - p) + mun_ref[...] * (nxt - p)

    w = RWKV_WIDTH
    r = pa[:, 0:w]
    k = pa[:, w:2 * w]
    v = pa[:, 2 * w:3 * w]
    w_lo = pa[:, 3 * w:3 * w + W_LORA]
    a_lo = pa[:, 3 * w + W_LORA:3 * w + W_LORA + A_LORA]
    g_lo = pa[:, 3 * w + W_LORA + A_LORA:]

    ones = _group_matrix(w, HEAD_DIM, 1.0)
    kk = k * kk_ref[...]
    ss = _dot_exact_rhs(kk * kk, ones)
    kk = kk / jnp.maximum(jnp.sqrt(ss), 1e-12)
    tw = jnp.tanh(w_lo)
    r_o[...] = r
    v_o[...] = v
    kk_o[...] = kk
    bonus = jnp.zeros_like(r)
    for d in range(2):
        z = w0_ref[d] + jnp.dot(tw, w2_ref[d], preferred_element_type=F32, precision=HIGHEST)
        nz = -z
        softplus = jnp.maximum(nz, 0.0) + jnp.log(1.0 + jnp.exp(-jnp.abs(nz)))
        w_log = -softplus - 0.5
        lw_o[d] = -jnp.exp(w_log)
        a = _sigmoid(a0_ref[d] + jnp.dot(a_lo, a2_ref[d], preferred_element_type=F32,
                                         precision=HIGHEST))
        k_d = k * (1.0 + (a - 1.0) * ka_ref[...])
        ad_o[d] = a
        kd_o[d] = k_d
        bonus = bonus + _dot_exact_rhs(r * k_d * rk_ref[...], ones) * v
    bonus_o[...] = bonus
    g_o[...] = jnp.dot(_sigmoid(g_lo), g2_ref[...], preferred_element_type=F32, precision=HIGHEST)


def _rwkv_prep(pa, bsz, seq, mu_prev, mu_next, w0, w2, a0, a2, k_k, k_a, r_k, g2, ts=256):
    t = pa.shape[0]
    w = RWKV_WIDTH
    nt = seq // ts
    hb = ts // HALO
    main = lambda b, i: (b * nt + i, 0)
    prev = lambda b, i: (jnp.maximum((b * nt + i) * hb - 1, 0), 0)
    nxt = lambda b, i: (jnp.minimum((b * nt + i + 1) * hb, t // HALO - 1), 0)
    c0 = lambda b, i: (0, 0)
    c3 = lambda b, i: (0, 0, 0)
    dmain = lambda b, i: (0, b * nt + i, 0)
    tok = jax.ShapeDtypeStruct((t, w), F32)
    tok2 = jax.ShapeDtypeStruct((2, t, w), F32)
    return pl.pallas_call(
        _rwkv_prep_kernel,
        grid=(bsz, nt),
        in_specs=[pl.BlockSpec((ts, A_PROJ), main),
                  pl.BlockSpec((HALO, A_PROJ), prev),
                  pl.BlockSpec((HALO, A_PROJ), nxt),
                  pl.BlockSpec((1, A_PROJ), c0),
                  pl.BlockSpec((1, A_PROJ), c0),
                  pl.BlockSpec((2, 1, w), c3),
                  pl.BlockSpec((2, W_LORA, w), c3),
                  pl.BlockSpec((2, 1, w), c3),
                  pl.BlockSpec((2, A_LORA, w), c3),
                  pl.BlockSpec((1, w), c0),
                  pl.BlockSpec((1, w), c0),
                  pl.BlockSpec((1, w), c0),
                  pl.BlockSpec((G_LORA, w), c0)],
        out_specs=[pl.BlockSpec((ts, w), main)] * 3
                  + [pl.BlockSpec((2, ts, w), dmain)] * 3
                  + [pl.BlockSpec((ts, w), main)] * 2,
        out_shape=[tok, tok, tok, tok2, tok2, tok2, tok, tok],
        compiler_params=_params(("parallel", "parallel"), VMEM_LIMIT),
        name="rwkv_prep",
    )(pa, pa, pa, mu_prev.reshape(1, -1), mu_next.reshape(1, -1), w0.reshape(2, 1, w), w2,
      a0.reshape(2, 1, w), a2, k_k.reshape(1, w), k_a.reshape(1, w), r_k.reshape(1, w), g2)


def _wkv_scaled(r, kd, v, kk, a, lw, reverse):
    c = r.shape[0]
    ri = lax.broadcasted_iota(I32, (c, c), 0)
    ci = lax.broadcasted_iota(I32, (c, c), 1)
    if reverse:
        incl, strict, last = ci >= ri, ci > ri, 0
    else:
        incl, strict, last = ci <= ri, ci < ri, c - 1
    cum = _dot_exact_lhs(incl.astype(BF16), lw)
    cum_last = cum[last:last + 1, :]
    w_end = jnp.exp(cum_last - cum)
    w_inv = jnp.exp(-cum)
    b = kk * a
    return dict(rt=r * jnp.exp(cum), kt=kd * w_inv, bt=b * w_inv, at=-kk * jnp.exp(cum - lw),
                bk_t=jnp.concatenate([b * w_end, kd * w_end], axis=0).T,
                w_c=jnp.exp(cum_last), v=v, incl=incl, strict=strict)


def _wkv_chunks(ops, states):
    c = ops[0]["rt"].shape[0]
    hd = HEAD_DIM
    n_heads = ops[0]["rt"].shape[1] // hd
    streams = [(d, h) for d in range(len(ops)) for h in range(n_heads)]
    ri = lax.broadcasted_iota(I32, (c, c), 0)
    ci = lax.broadcasted_iota(I32, (c, c), 1)
    eye = (ri == ci).astype(F32)
    zeros = jnp.zeros((c, hd), F32)

    def head(d, h, name):
        return ops[d][name][:, h * hd:(h + 1) * hd]

    ll = [_dot_nt(jnp.concatenate([head(d, h, "at"), head(d, h, "rt")], axis=0),
                  jnp.concatenate([head(d, h, "bt"), head(d, h, "kt")], axis=0))
          for d, h in streams]
    l_ab = [jnp.where(ops[d]["strict"], m[:c, :c], 0.0) for (d, h), m in zip(streams, ll)]
    l_ak = [jnp.where(ops[d]["strict"], m[:c, c:], 0.0) for (d, h), m in zip(streams, ll)]
    l_rr = [jnp.concatenate([jnp.where(ops[d]["incl"], m[c:, :c], 0.0),
                             jnp.where(ops[d]["incl"], m[c:, c:], 0.0)], axis=1)
            for (d, h), m in zip(streams, ll)]
    tinv = [eye + m for m in l_ab]
    lp = l_ab
    for _ in range(max(1, (c - 1).bit_length() - 1)):
        lp = [_dot(m, m) for m in lp]
        tinv = [t + _dot(m, t) for m, t in zip(lp, tinv)]
    lakv = [_dot(m, head(d, h, "v")) for (d, h), m in zip(streams, l_ak)]
    x = [_dot(t, jnp.concatenate([head(d, h, "at"), u], axis=1))
         for (d, h), t, u in zip(streams, tinv, lakv)]
    o2 = [_dot(jnp.concatenate([lr, ops[d]["bk_t"][h * hd:(h + 1) * hd, :]], axis=0),
               jnp.concatenate([xx, jnp.concatenate([zeros, head(d, h, "v")], axis=1)], axis=0))
          for (d, h), lr, xx in zip(streams, l_rr, x)]
    o3 = [_dot(jnp.concatenate([head(d, h, "rt") + m[:c, :hd],
                                eye[:hd, :hd] * head(d, h, "w_c") + m[c:, :hd]], axis=0),
               states[d][:, h * hd:(h + 1) * hd])
          for (d, h), m in zip(streams, o2)]
    ys, sts = [], []
    for d in range(len(ops)):
        sel = [(m2, m3) for (dd, h), m2, m3 in zip(streams, o2, o3) if dd == d]
        ys.append(jnp.concatenate([m3[:c] + m2[:c, hd:] for m2, m3 in sel], axis=1))
        sts.append(jnp.concatenate([m3[c:] + m2[c:, hd:] for m2, m3 in sel], axis=1))
    return ys, sts


def _rwkv_scan_kernel(rf, vf, kkf, kdf, adf, lwf, rb, vb, kkb, kdb, adb, lwb, yf_o, yb_o, st_ref):
    @pl.when(pl.program_id(1) == 0)
    def _():
        st_ref[...] = jnp.zeros_like(st_ref)

    ops = [_wkv_scaled(rf[...], kdf[0], vf[...], kkf[...], adf[0], lwf[0], False),
           _wkv_scaled(rb[...], kdb[0], vb[...], kkb[...], adb[0], lwb[0], True)]
    ys, sts = _wkv_chunks(ops, [st_ref[0], st_ref[1]])
    yf_o[...] = ys[0]
    yb_o[...] = ys[1]
    st_ref[0] = sts[0]
    st_ref[1] = sts[1]


def _rwkv_scan(r, v, kk, kd, ad, lw, bsz, seq):
    t, w = r.shape
    nc = seq // CHUNK
    fwd = lambda b, i: (b * nc + i, 0)
    bwd = lambda b, i: (b * nc + nc - 1 - i, 0)
    fwd0 = lambda b, i: (0, b * nc + i, 0)
    bwd1 = lambda b, i: (1, b * nc + nc - 1 - i, 0)
    tok = pl.BlockSpec((CHUNK, w), fwd)
    tokb = pl.BlockSpec((CHUNK, w), bwd)
    dir0 = pl.BlockSpec((1, CHUNK, w), fwd0)
    dir1 = pl.BlockSpec((1, CHUNK, w), bwd1)
    return pl.pallas_call(
        _rwkv_scan_kernel,
        grid=(bsz, nc),
        in_specs=[tok, tok, tok, dir0, dir0, dir0, tokb, tokb, tokb, dir1, dir1, dir1],
        out_specs=[tok, tokb],
        out_shape=[jax.ShapeDtypeStruct((t, w), F32)] * 2,
        scratch_shapes=[pltpu.VMEM((2, HEAD_DIM, w), F32)],
        compiler_params=_params(("parallel", "arbitrary"), VMEM_LIMIT),
        name="rwkv_scan",
    )(r, v, kk, kd, ad, lw, r, v, kk, kd, ad, lw)


def _conv_kernel(pm_ref, pp_ref, pn_ref, cw_ref, cb_ref, g_ref, b_ref, o_ref, hbuf):
    i = pl.program_id(1)
    n = pl.num_programs(1)
    ts = pm_ref.shape[0]
    cwid = CONV_WIDTH

    def glu(p):
        return p[:, :cwid] * _sigmoid(p[:, cwid:])

    hbuf[0:HALO, :] = jnp.where(i > 0, glu(pp_ref[...]), 0.0)
    hbuf[HALO:HALO + ts, :] = glu(pm_ref[...])
    hbuf[HALO + ts:2 * HALO + ts, :] = jnp.where(i < n - 1, glu(pn_ref[...]), 0.0)
    acc = jnp.zeros((ts, cwid), F32)
    for k in range(CONV_KERNEL):
        off = HALO - CONV_PAD + k
        acc = acc + cw_ref[k:k + 1, :] * hbuf[off:off + ts, :]
    acc = acc + cb_ref[...]
    avg = _group_matrix(cwid, cwid // CONV_GROUPS, 1.0 / (cwid // CONV_GROUPS))
    mu = _dot_exact_rhs(acc, avg)
    xc = acc - mu
    var = _dot_exact_rhs(xc * xc, avg)
    y = xc * lax.rsqrt(var + GN_EPS) * g_ref[...] + b_ref[...]
    o_ref[...] = y * _sigmoid(y)


def _conformer_conv(pb, bsz, seq, conv_w, conv_b, gn_g, gn_b, ts=512):
    t = pb.shape[0]
    cwid = CONV_WIDTH
    nt = seq // ts
    hb = ts // HALO
    main = lambda b, i: (b * nt + i, 0)
    prev = lambda b, i: (jnp.maximum((b * nt + i) * hb - 1, 0), 0)
    nxt = lambda b, i: (jnp.minimum((b * nt + i + 1) * hb, t // HALO - 1), 0)
    c0 = lambda b, i: (0, 0)
    return pl.pallas_call(
        _conv_kernel,
        grid=(bsz, nt),
        in_specs=[pl.BlockSpec((ts, 2 * cwid), main),
                  pl.BlockSpec((HALO, 2 * cwid), prev),
                  pl.BlockSpec((HALO, 2 * cwid), nxt),
                  pl.BlockSpec((CONV_KERNEL, cwid), c0),
                  pl.BlockSpec((1, cwid), c0),
                  pl.BlockSpec((1, cwid), c0),
                  pl.BlockSpec((1, cwid), c0)],
        out_specs=pl.BlockSpec((ts, cwid), main),
        out_shape=jax.ShapeDtypeStruct((t, cwid), F32),
        scratch_shapes=[pltpu.VMEM((ts + 2 * HALO, cwid), F32)],
        compiler_params=_params(("parallel", "parallel"), VMEM_LIMIT),
        name="conformer_conv",
    )(pb, pb, pb, conv_w, conv_b.reshape(1, cwid), gn_g.reshape(1, cwid), gn_b.reshape(1, cwid))


def _mix_out_kernel(yf_ref, yb_ref, bonus_ref, g_ref, cv_ref, x_ref, lg_ref, lb_ref, w_ref,
                    ng_ref, gt_ref, o_ref):
    w = RWKV_WIDTH
    wkv = yf_ref[...] + yb_ref[...]
    avg = _group_matrix(w, HEAD_DIM, 1.0 / HEAD_DIM)
    mu = _dot_exact_rhs(wkv, avg)
    xc = wkv - mu
    var = _dot_exact_rhs(xc * xc, avg)
    y = xc * lax.rsqrt(var + LNX_EPS) * lg_ref[...] + lb_ref[...] + bonus_ref[...]
    ya = y * g_ref[...]
    cat = jnp.concatenate([ya.astype(BF16), cv_ref[...].astype(BF16)], axis=1)
    z = _dot(cat, w_ref[...])
    o_ref[...] = x_ref[...] + gt_ref[0] * _rms_norm(z, ng_ref[...])


def _mix_out(yf, yb, bonus, g, cv, x2, lnx_g, lnx_b, w_out_bf16, norm_g, gt, seq, tm=256):
    t, d = x2.shape
    w = RWKV_WIDTH
    tpb = seq // tm
    row = lambda i: (i, 0)
    c0 = lambda i: (0, 0)
    half = pl.BlockSpec((tm, w), row)
    return pl.pallas_call(
        _mix_out_kernel,
        grid=(t // tm,),
        in_specs=[half, half, half, half, half,
                  pl.BlockSpec((tm, d), row),
                  pl.BlockSpec((1, w), c0),
                  pl.BlockSpec((1, w), c0),
                  pl.BlockSpec((d, d), c0),
                  pl.BlockSpec((1, d), c0),
                  pl.BlockSpec((1, 1, d), lambda i: (i // tpb, 0, 0))],
        out_specs=pl.BlockSpec((tm, d), row),
        out_shape=jax.ShapeDtypeStruct((t, d), F32),
        compiler_params=_params(("parallel",), VMEM_LIMIT),
        name="mix_out",
    )(yf, yb, bonus, g, cv, x2, lnx_g.reshape(1, w), lnx_b.reshape(1, w), w_out_bf16,
      norm_g.reshape(1, d), gt)


def _attn_kernel(sink_ref, q_ref, kp_ref, kc_ref, kn_ref, vp_ref, vc_ref, vn_ref, bias_ref, o_ref):
    i = pl.program_id(1)
    n = pl.num_programs(1)
    blk = ATTN_BLOCK
    hd = HEAD_DIM
    k = jnp.concatenate([kp_ref[...], kc_ref[...], kn_ref[...]], axis=0).astype(BF16)
    v = jnp.concatenate([vp_ref[...], vc_ref[...], vn_ref[...]], axis=0).astype(BF16)
    col = lax.broadcasted_iota(I32, (1, 3 * blk), 1)
    valid = ((col >= blk) | (i > 0)) & ((col < 2 * blk) | (i < n - 1))
    scale = hd ** -0.5
    kvs = range(ATTN_KV_HEADS)
    qs = [jnp.concatenate([q_ref[:, (kvh * ATTN_GROUP + g) * hd:(kvh * ATTN_GROUP + g + 1) * hd]
                           for g in range(ATTN_GROUP)], axis=0).astype(BF16) for kvh in kvs]
    scs = [_dot_nt(qs[kvh], k[:, kvh * hd:(kvh + 1) * hd]) for kvh in kvs]
    ps, denoms = [], []
    for kvh in kvs:
        sc = jnp.where(valid, scs[kvh] * scale + bias_ref[kvh], NEG)
        sink = sink_ref[kvh]
        m = jnp.maximum(jnp.max(sc, axis=-1, keepdims=True), sink)
        p = jnp.exp(sc - m)
        denoms.append(jnp.sum(p, axis=-1, keepdims=True) + jnp.exp(sink - m))
        ps.append(p.astype(BF16))
    for kvh in kvs:
        o = _dot(ps[kvh], v[:, kvh * hd:(kvh + 1) * hd]) / denoms[kvh]
        for g in range(ATTN_GROUP):
            h = kvh * ATTN_GROUP + g
            o_ref[:, h * hd:(h + 1) * hd] = o[g * blk:(g + 1) * blk, :]


def _attention(qkv, bias, sink, bsz, seq):
    t = qkv.shape[0]
    rows = ATTN_GROUP * ATTN_BLOCK
    bias = bias.reshape(ATTN_KV_HEADS, rows, 3 * ATTN_BLOCK)
    sink = jnp.repeat(sink.astype(F32), ATTN_BLOCK).reshape(ATTN_KV_HEADS, rows, 1)
    blk = ATTN_BLOCK
    nb = seq // blk
    kcol = Q_WIDTH // KV_WIDTH
    vcol = kcol + 1
    cur = lambda b, i: b * nb + i
    prv = lambda b, i: b * nb + jnp.maximum(i - 1, 0)
    nxt = lambda b, i: b * nb + jnp.minimum(i + 1, nb - 1)
    kv = lambda rowf, c: pl.BlockSpec((blk, KV_WIDTH), lambda b, i: (rowf(b, i), c))
    return pl.pallas_call(
        _attn_kernel,
        grid=(bsz, nb),
        in_specs=[pl.BlockSpec((ATTN_KV_HEADS, rows, 1), lambda b, i: (0, 0, 0)),
                  pl.BlockSpec((blk, Q_WIDTH), lambda b, i: (cur(b, i), 0)),
                  kv(prv, kcol), kv(cur, kcol), kv(nxt, kcol),
                  kv(prv, vcol), kv(cur, vcol), kv(nxt, vcol),
                  pl.BlockSpec((ATTN_KV_HEADS, rows, 3 * blk), lambda b, i: (0, 0, 0))],
        out_specs=pl.BlockSpec((blk, Q_WIDTH), lambda b, i: (cur(b, i), 0)),
        out_shape=jax.ShapeDtypeStruct((t, Q_WIDTH), F32),
        compiler_params=_params(("parallel", "parallel"), VMEM_LIMIT),
        name="attention",
    )(sink, qkv, qkv, qkv, qkv, qkv, qkv, qkv, bias)


def _t5_bucket(rel):
    n = -rel
    half = N_BUCKETS // 2
    ret = jnp.where(n < 0, half, 0)
    n = jnp.abs(n)
    max_exact = half // 2
    nf = jnp.maximum(n, 1).astype(F32)
    large = max_exact + (jnp.log(nf / max_exact) / jnp.log(MAX_DISTANCE / max_exact)
                         * (half - max_exact)).astype(I32)
    large = jnp.minimum(large, half - 1)
    return ret + jnp.where(n < max_exact, n, large)


def _attn_bias(rel_bias):
    blk = ATTN_BLOCK
    rel = jnp.arange(3 * blk)[None, :] - blk - jnp.arange(blk)[:, None]
    bias = jnp.transpose(rel_bias[_t5_bucket(rel)].astype(F32), (2, 0, 1))
    return jnp.where(jnp.abs(rel) <= WINDOW, bias, NEG)


def _out_proj_kernel(y_ref, x_ref, w_ref, ng_ref, gt_ref, o_ref):
    z = _dot(y_ref[...].astype(BF16), w_ref[...])
    o_ref[...] = x_ref[...] + gt_ref[0] * _rms_norm(z, ng_ref[...])


def _out_proj(y, x2, w_bf16, norm_g, gt, seq, tm=256):
    t, d = x2.shape
    tpb = seq // tm
    row = lambda i: (i, 0)
    c0 = lambda i: (0, 0)
    return pl.pallas_call(
        _out_proj_kernel,
        grid=(t // tm,),
        in_specs=[pl.BlockSpec((tm, d), row),
                  pl.BlockSpec((tm, d), row),
                  pl.BlockSpec((d, d), c0),
                  pl.BlockSpec((1, d), c0),
                  pl.BlockSpec((1, 1, d), lambda i: (i // tpb, 0, 0))],
        out_specs=pl.BlockSpec((tm, d), row),
        out_shape=jax.ShapeDtypeStruct((t, d), F32),
        compiler_params=_params(("parallel",), VMEM_LIMIT),
        name="out_proj",
    )(y, x2, w_bf16, norm_g.reshape(1, d), gt)


def _router_kernel(x_ref, g_ref, sc_ref, sh_ref, rw_ref, rb_ref, idx_o, gate_o, rank_o, cnt_o,
                   cnt_ref):
    @pl.when(pl.program_id(0) == 0)
    def _():
        cnt_ref[...] = jnp.zeros_like(cnt_ref)

    h = _norm_mod(x_ref[...], g_ref[...], sc_ref[0], sh_ref[0])
    tm = h.shape[0]
    ne = N_EXPERTS
    logits = lax.dot_general(rw_ref[...], h, (((1,), (1,)), ((), ())),
                             preferred_element_type=F32, precision=HIGHEST) + rb_ref[...]
    eidx = lax.broadcasted_iota(I32, (ne, tm), 0)
    vals, sels = [], []
    for k in range(TOP_K):
        m = jnp.max(logits, axis=0, keepdims=True)
        idx = jnp.min(jnp.where(logits == m, eidx, ne), axis=0, keepdims=True)
        sel = eidx == idx
        vals.append(m)
        sels.append(sel)
        idx_o[k:k + 1, :] = idx
        logits = jnp.where(sel, -jnp.inf, logits)
    es = [jnp.exp(vk - vals[0]) for vk in vals]
    tot = es[0] + es[1] + es[2] + es[3]
    for k in range(TOP_K):
        gate_o[k:k + 1, :] = es[k] / tot
    onehot = (sels[0] | sels[1] | sels[2] | sels[3])
    before = (lax.broadcasted_iota(I32, (tm, tm), 0) < lax.broadcasted_iota(I32, (tm, tm), 1))
    cum = _dot(onehot.astype(BF16), before.astype(BF16)) + cnt_ref[:, 0:1]
    for k in range(TOP_K):
        rank_o[k:k + 1, :] = jnp.sum(jnp.where(sels[k], cum, 0.0), axis=0,
                                     keepdims=True).astype(I32)
    cnt_ref[...] = cnt_ref[...] + jnp.sum(onehot.astype(F32), axis=1, keepdims=True)
    cnt_o[...] = cnt_ref[...]


def _router(x2, g, sc, sh, router_w_t, router_b, seq, tm=512):
    t, d = x2.shape
    ne = N_EXPERTS
    tpb = seq // tm
    per_b = lambda i: (i // tpb, 0, 0)
    c0 = lambda i: (0, 0)
    col = lambda i: (0, i)
    return pl.pallas_call(
        _router_kernel,
        grid=(t // tm,),
        in_specs=[pl.BlockSpec((tm, d), lambda i: (i, 0)),
                  pl.BlockSpec((1, d), c0),
                  pl.BlockSpec((1, 1, d), per_b),
                  pl.BlockSpec((1, 1, d), per_b),
                  pl.BlockSpec((ne, d), c0),
                  pl.BlockSpec((ne, 1), c0)],
        out_specs=[pl.BlockSpec((TOP_K, tm), col),
                   pl.BlockSpec((TOP_K, tm), col),
                   pl.BlockSpec((TOP_K, tm), col),
                   pl.BlockSpec((ne, 128), c0)],
        out_shape=[jax.ShapeDtypeStruct((TOP_K, t), I32),
                   jax.ShapeDtypeStruct((TOP_K, t), F32),
                   jax.ShapeDtypeStruct((TOP_K, t), I32),
                   jax.ShapeDtypeStruct((ne, 128), F32)],
        scratch_shapes=[pltpu.VMEM((ne, 128), F32)],
        compiler_params=_params(("arbitrary",), VMEM_LIMIT),
        name="router",
    )(x2, g.reshape(1, d), sc, sh, router_w_t, router_b.reshape(ne, 1))


def _pack_rows(h):
    half = h.shape[1] // 2
    hi = lax.bitcast_convert_type(h[:, :half].astype(BF16).astype(F32), U32)
    lo = lax.bitcast_convert_type(h[:, half:].astype(BF16).astype(F32), U32)
    return (hi & jnp.uint32(0xFFFF0000)) | (lo >> 16)


def _unpack_rows(p):
    hi = lax.bitcast_convert_type(p & jnp.uint32(0xFFFF0000), F32)
    lo = lax.bitcast_convert_type(p << 16, F32)
    return jnp.concatenate([hi, lo], axis=1)


def _tile_indices(dest_hbm, dest_smem, sem):
    i = pl.program_id(0)
    n = pl.num_programs(0)
    slot = i % 2

    def fetch(step, s):
        return pltpu.make_async_copy(dest_hbm.at[step], dest_smem.at[s], sem)

    @pl.when(i == 0)
    def _():
        fetch(0, 0).start()

    fetch(i, slot).wait()

    @pl.when(i + 1 < n)
    def _():
        fetch(i + 1, 1 - slot).start()

    return slot


def _zero_block_kernel(blk_ref, o_ref):
    del blk_ref
    o_ref[...] = jnp.zeros_like(o_ref)


def _zero_last_blocks(last_blk, n_slots, width):
    bm = MOE_BLOCK
    grid_spec = pltpu.PrefetchScalarGridSpec(
        num_scalar_prefetch=1,
        grid=(last_blk.shape[0],),
        in_specs=[],
        out_specs=pl.BlockSpec((bm, width), lambda e, blk: (blk[e], 0)),
    )
    return pl.pallas_call(
        _zero_block_kernel,
        grid_spec=grid_spec,
        out_shape=jax.ShapeDtypeStruct((n_slots, width), U32),
        compiler_params=_params(("arbitrary",)),
        name="moe_zero_pad",
    )(last_blk)


def _dispatch_kernel(x_ref, g_ref, sc_ref, sh_ref, dest_hbm, xs_in, xs_hbm, hbuf, dest_smem, sems):
    del xs_in
    tm = x_ref.shape[0]
    hbuf[...] = _pack_rows(_norm_mod(x_ref[...], g_ref[...], sc_ref[0], sh_ref[0]))
    slot = _tile_indices(dest_hbm, dest_smem, sems.at[0])

    def issue(grp, carry):
        for u in range(ROW_UNROLL):
            row = grp * ROW_UNROLL + u
            for k in range(TOP_K):
                pltpu.make_async_copy(hbuf.at[pl.ds(row, 1), :],
                                      xs_hbm.at[pl.ds(dest_smem[slot, k * tm + row], 1), :],
                                      sems.at[1]).start()
        return carry

    lax.fori_loop(0, tm // ROW_UNROLL, issue, 0)

    for k in range(TOP_K):
        pltpu.make_async_copy(hbuf, xs_hbm.at[pl.ds(0, tm), :], sems.at[1]).wait()


def _dispatch(x2, g, sc, sh, dest_tiles, xs_init, seq, tm):
    t, d = x2.shape
    tpb = seq // tm
    per_b = lambda i: (i // tpb, 0, 0)
    return pl.pallas_call(
        _dispatch_kernel,
        grid=(t // tm,),
        in_specs=[pl.BlockSpec((tm, d), lambda i: (i, 0)),
                  pl.BlockSpec((1, d), lambda i: (0, 0)),
                  pl.BlockSpec((1, 1, d), per_b),
                  pl.BlockSpec((1, 1, d), per_b),
                  pl.BlockSpec(memory_space=pl.ANY),
                  pl.BlockSpec(memory_space=pl.ANY)],
        out_specs=pl.BlockSpec(memory_space=pl.ANY),
        out_shape=jax.ShapeDtypeStruct(xs_init.shape, U32),
        scratch_shapes=[pltpu.VMEM((tm, d // 2), U32),
                        pltpu.SMEM((2, TOP_K * tm), I32),
                        pltpu.SemaphoreType.DMA((2,))],
        input_output_aliases={5: 0},
        compiler_params=_params(("arbitrary",), VMEM_LIMIT),
        name="moe_dispatch",
    )(x2, g.reshape(1, d), sc, sh, dest_tiles, xs_init)


def _expert_kernel(be_ref, first_ref, used_ref, xs_ref, w1_ref, b1_ref, w2_ref, b2_ref, ys_ref,
                   w1p, w2p):
    i = pl.program_id(0)
    f2 = w1_ref.shape[3]
    pw = 256

    @pl.when((first_ref[i] == 1) & (i < used_ref[0]))
    def _():
        r = lax.broadcasted_iota(I32, (pw, pw), 0)
        c = lax.broadcasted_iota(I32, (pw, pw), 1)
        src = jnp.where(c < pw // 2, 2 * c, 2 * (c - pw // 2) + 1)
        perm = (r == src).astype(BF16)
        for j in range(f2 // pw):
            w1p[:, j * pw:(j + 1) * pw] = _dot(w1_ref[0, 0, :, j * pw:(j + 1) * pw].astype(BF16),
                                               perm).astype(BF16)
        w2p[...] = w2_ref[0, 0].astype(BF16)

    @pl.when(i < used_ref[0])
    def _():
        x = _unpack_rows(xs_ref[...]).astype(BF16)
        hcat = _dot(x, w1p[...]) + b1_ref[0, 0]
        acts = []
        for j in range(f2 // pw):
            x_glu = jnp.minimum(hcat[:, j * pw:j * pw + pw // 2], SWIGLU_LIMIT)
            x_lin = jnp.clip(hcat[:, j * pw + pw // 2:(j + 1) * pw], -SWIGLU_LIMIT, SWIGLU_LIMIT)
            acts.append(x_glu * _sigmoid(SWIGLU_ALPHA * x_glu) * (x_lin + 1.0))
        act = jnp.concatenate(acts, axis=1).astype(BF16)
        ys_ref[...] = _pack_rows(_dot(act, w2p[...]) + b2_ref[0, 0])

    @pl.when(i >= used_ref[0])
    def _():
        ys_ref[...] = jnp.zeros_like(ys_ref)


def _experts(xs, blk_expert, blk_first, n_used, layer, w1, b1_perm, w2, b2):
    n_slots, dp = xs.shape
    _, ne, d, f2 = w1.shape
    f = w2.shape[2]
    bm = MOE_BLOCK
    wmap = lambda i, be, fi, us: (layer, be[i], 0, 0)
    grid_spec = pltpu.PrefetchScalarGridSpec(
        num_scalar_prefetch=3,
        grid=(n_slots // bm,),
        in_specs=[pl.BlockSpec((bm, dp), lambda i, be, fi, us: (i, 0)),
                  pl.BlockSpec((1, 1, d, f2), wmap),
                  pl.BlockSpec((1, 1, 1, f2), wmap),
                  pl.BlockSpec((1, 1, f, d), wmap),
                  pl.BlockSpec((1, 1, 1, d), wmap)],
        out_specs=pl.BlockSpec((bm, dp), lambda i, be, fi, us: (i, 0)),
        scratch_shapes=[pltpu.VMEM((d, f2), BF16), pltpu.VMEM((f, d), BF16)],
    )
    depth = w1.shape[0]
    return pl.pallas_call(
        _expert_kernel,
        grid_spec=grid_spec,
        out_shape=jax.ShapeDtypeStruct((n_slots, dp), U32),
        compiler_params=_params(("arbitrary",), VMEM_LIMIT),
        name="moe_experts",
    )(blk_expert, blk_first, n_used, xs, w1, b1_perm.reshape(depth, ne, 1, f2), w2,
      b2.reshape(depth, ne, 1, d))


def _combine_kernel(x_ref, gate_ref, ng_ref, gt_ref, dest_hbm, ys_hbm, o_ref, ybuf, dest_smem, sems):
    tm = x_ref.shape[0]
    slot = _tile_indices(dest_hbm, dest_smem, sems.at[0])

    def issue(grp, carry):
        for u in range(TOP_K * ROW_UNROLL):
            j = grp * (TOP_K * ROW_UNROLL) + u
            pltpu.make_async_copy(ys_hbm.at[pl.ds(dest_smem[slot, j], 1), :],
                                  ybuf.at[pl.ds(j, 1), :], sems.at[1]).start()
        return carry

    lax.fori_loop(0, tm // ROW_UNROLL, issue, 0)
    pltpu.make_async_copy(ys_hbm.at[pl.ds(0, TOP_K * tm), :], ybuf, sems.at[1]).wait()
    y = jnp.zeros(x_ref.shape, F32)
    for k in range(TOP_K):
        y = y + gate_ref[:, k:k + 1] * _unpack_rows(ybuf[k * tm:(k + 1) * tm, :])
    o_ref[...] = x_ref[...] + gt_ref[0] * _rms_norm(y, ng_ref[...])


def _combine(x2, gates_tk, norm_g, gt, dest_tiles, ys, seq, tm):
    t, d = x2.shape
    tpb = seq // tm
    return pl.pallas_call(
        _combine_kernel,
        grid=(t // tm,),
        in_specs=[pl.BlockSpec((tm, d), lambda i: (i, 0)),
                  pl.BlockSpec((tm, TOP_K), lambda i: (i, 0)),
                  pl.BlockSpec((1, d), lambda i: (0, 0)),
                  pl.BlockSpec((1, 1, d), lambda i: (i // tpb, 0, 0)),
                  pl.BlockSpec(memory_space=pl.ANY),
                  pl.BlockSpec(memory_space=pl.ANY)],
        out_specs=pl.BlockSpec((tm, d), lambda i: (i, 0)),
        out_shape=jax.ShapeDtypeStruct((t, d), F32),
        scratch_shapes=[pltpu.VMEM((TOP_K * tm, d // 2), U32),
                        pltpu.SMEM((2, TOP_K * tm), I32),
                        pltpu.SemaphoreType.DMA((2,))],
        compiler_params=_params(("arbitrary",), VMEM_LIMIT),
        name="moe_combine",
    )(x2, gates_tk, norm_g.reshape(1, d), gt, dest_tiles, ys)


def _moe(x2, norm_pre, sc, sh, norm_post, gt, router_w, router_b, layer, w1, b1_perm, w2, b2, seq,
         tm=512):
    t, d = x2.shape
    ne = N_EXPERTS
    bm = MOE_BLOCK
    top_idx, gates, rank, cnt = _router(x2, norm_pre, sc, sh, router_w.T, router_b, seq)
    counts = cnt[:, 0].astype(I32)
    padded = ((counts + bm - 1) // bm) * bm
    padded_end = jnp.cumsum(padded)
    padded_start = padded_end - padded
    start_of = jnp.sum(jnp.where(top_idx[..., None] == jnp.arange(ne), padded_start, 0), axis=-1)
    dest = start_of + rank
    n_slots = t * TOP_K + ne * bm
    n_blocks = n_slots // bm
    blk_row = jnp.arange(n_blocks, dtype=I32) * bm
    blk_expert = jnp.minimum(jnp.sum((padded_end[None, :] <= blk_row[:, None]).astype(I32), axis=1),
                             ne - 1)
    blk_first = jnp.concatenate([jnp.ones((1,), I32),
                                 (blk_expert[1:] != blk_expert[:-1]).astype(I32)])
    n_used = (padded_end[-1:] // bm).astype(I32)
    dest_tiles = dest.reshape(TOP_K, t // tm, tm).transpose(1, 0, 2).reshape(t // tm, TOP_K * tm)
    last_blk = jnp.maximum(padded_end // bm - 1, 0).astype(I32)
    xs = _dispatch(x2, norm_pre, sc, sh, dest_tiles, _zero_last_blocks(last_blk, n_slots, d // 2),
                   seq, tm)
    ys = _experts(xs, blk_expert, blk_first, n_used, layer, w1, b1_perm, w2, b2)
    return _combine(x2, gates.T, norm_post, gt, dest_tiles, ys, seq, tm)


def kernel(x, c, mod_w, mod_b, norm_mix_pre, norm_mix_post, norm_ffn_pre, norm_ffn_post, ab_w_in, ab_mu_prev, ab_mu_next, rwkv_w0, rwkv_w2, rwkv_a0, rwkv_a2, rwkv_k_k, rwkv_k_a, rwkv_r_k, rwkv_lnx_g, rwkv_lnx_b, rwkv_g2, conv_w, conv_b, conv_gn_g, conv_gn_b, ab_w_out, attn_w_qkv, attn_sink, attn_w_o, rel_bias, router_w, router_b, exp_w1, exp_b1, exp_w2, exp_b2):
    bsz, seq, d = x.shape
    depth = mod_w.shape[0]
    t = bsz * seq
    x2 = x.reshape(t, d)
    mod = _mod_all(c, mod_w, mod_b)
    bias = _attn_bias(rel_bias)
    ne, f2 = exp_b1.shape[1:]
    b1_perm = exp_b1.reshape(depth, ne, f2 // 256, 128, 2).swapaxes(-1, -2).reshape(depth, ne, f2)
    for layer in range(depth):
        j = layer // 2
        sh1, sc1, gt1, sh2, sc2, gt2 = [m.reshape(bsz, 1, d) for m in jnp.split(mod[layer], 6, axis=-1)]
        if layer % 2 == 0:
            pa, pb = _in_proj(x2, norm_mix_pre[layer], sc1, sh1, ab_w_in[j].astype(BF16),
                              (A_PROJ, 2 * CONV_WIDTH), seq)
            r, v, kk, kd, ad, lw, bonus, g = _rwkv_prep(
                pa, bsz, seq, ab_mu_prev[j], ab_mu_next[j], rwkv_w0[j], rwkv_w2[j], rwkv_a0[j],
                rwkv_a2[j], rwkv_k_k[j], rwkv_k_a[j], rwkv_r_k[j], rwkv_g2[j])
            yf, yb = _rwkv_scan(r, v, kk, kd, ad, lw, bsz, seq)
            cv = _conformer_conv(pb, bsz, seq, conv_w[j], conv_b[j], conv_gn_g[j], conv_gn_b[j])
            x2 = _mix_out(yf, yb, bonus, g, cv, x2, rwkv_lnx_g[j], rwkv_lnx_b[j],
                          ab_w_out[j].astype(BF16), norm_mix_post[layer], gt1, seq)
        else:
            (qkv,) = _in_proj(x2, norm_mix_pre[layer], sc1, sh1, attn_w_qkv[j].astype(BF16),
                              (Q_WIDTH + 2 * KV_WIDTH,), seq)
            y = _attention(qkv, bias, attn_sink[j], bsz, seq)
            x2 = _out_proj(y, x2, attn_w_o[j].astype(BF16), norm_mix_post[layer], gt1, seq)
        x2 = _moe(x2, norm_ffn_pre[layer], sc2, sh2, norm_ffn_post[layer], gt2, router_w[layer],
                  router_b[layer], layer, exp_w1, b1_perm, exp_w2, exp_b2, seq)
    return x2.reshape(bsz, seq, d)
```

```python
import functools

import jax
import jax.numpy as jnp
from jax import lax
from jax.experimental import pallas as pl
from jax.experimental.pallas import tpu as pltpu

F32 = jnp.float32
BF16 = jnp.bfloat16
I32 = jnp.int32
U32 = jnp.uint32
HIGHEST = lax.Precision.HIGHEST

D_MODEL = 1024
HEAD_DIM = 64
RWKV_WIDTH = 512
RWKV_HEADS = RWKV_WIDTH // HEAD_DIM
W_LORA = 64
A_LORA = 64
G_LORA = 128
A_PROJ = 3 * RWKV_WIDTH + W_LORA + A_LORA + G_LORA
CONV_WIDTH = D_MODEL - RWKV_WIDTH
CONV_KERNEL = 31
CONV_PAD = CONV_KERNEL // 2
CONV_GROUPS = 8
ATTN_HEADS = 16
ATTN_KV_HEADS = 2
ATTN_GROUP = ATTN_HEADS // ATTN_KV_HEADS
Q_WIDTH = ATTN_HEADS * HEAD_DIM
KV_WIDTH = ATTN_KV_HEADS * HEAD_DIM
ATTN_BLOCK = 128
WINDOW = 128
N_BUCKETS = 32
MAX_DISTANCE = 128
N_EXPERTS = 32
TOP_K = 4
SWIGLU_LIMIT = 7.0
SWIGLU_ALPHA = 1.702
RMS_EPS = 1e-6
GN_EPS = 1e-5
LNX_EPS = 64e-5
NEG = -1e30

CHUNK = 64
HALO = 16
MOE_BLOCK = 512
ROW_UNROLL = 8
VMEM_LIMIT = 56 * 1024 * 1024


def _params(sem, vmem=None):
    kw = dict(dimension_semantics=sem)
    if vmem is not None:
        kw["vmem_limit_bytes"] = vmem
    return pltpu.CompilerParams(**kw)


def _sigmoid(x):
    return 1.0 / (1.0 + jnp.exp(-x))


def _dot(a, b):
    return jnp.dot(a, b, preferred_element_type=F32)


def _dot_nt(a, b):
    return lax.dot_general(a, b, (((1,), (1,)), ((), ())), preferred_element_type=F32)


def _dot_exact_lhs(a_bf16, x):
    xh = x.astype(BF16)
    xl = (x - xh.astype(F32)).astype(BF16)
    return _dot(a_bf16, xh) + _dot(a_bf16, xl)


def _dot_exact_rhs(x, a_bf16):
    xh = x.astype(BF16)
    xl = (x - xh.astype(F32)).astype(BF16)
    return _dot(xh, a_bf16) + _dot(xl, a_bf16)


def _group_matrix(width, group, value):
    r = lax.broadcasted_iota(I32, (width, width), 0) // group
    c = lax.broadcasted_iota(I32, (width, width), 1) // group
    return jnp.where(r == c, value, 0.0).astype(BF16)


def _rms_norm(x, g):
    ms = jnp.mean(x * x, axis=-1, keepdims=True)
    return x * lax.rsqrt(ms + RMS_EPS) * g


def _norm_mod(x, g, sc, sh):
    return _rms_norm(x, g) * (1.0 + sc) + sh


def _mod_kernel(c_ref, w_ref, b_ref, o_ref):
    c = c_ref[...]
    ca = c * _sigmoid(c)
    o_ref[0] = jnp.dot(ca, w_ref[0], preferred_element_type=F32, precision=HIGHEST) + b_ref[0]


def _mod_all(c, mod_w, mod_b):
    depth, d, n = mod_w.shape
    bsz = c.shape[0]
    tn = 1536
    return pl.pallas_call(
        _mod_kernel,
        grid=(depth, n // tn),
        in_specs=[pl.BlockSpec((bsz, d), lambda l, j: (0, 0)),
                  pl.BlockSpec((1, d, tn), lambda l, j: (l, 0, j)),
                  pl.BlockSpec((1, 1, tn), lambda l, j: (l, 0, j))],
        out_specs=pl.BlockSpec((1, bsz, tn), lambda l, j: (l, 0, j)),
        out_shape=jax.ShapeDtypeStruct((depth, bsz, n), F32),
        compiler_params=_params(("parallel", "parallel"), VMEM_LIMIT),
        name="mod",
    )(c, mod_w, mod_b.reshape(depth, 1, n))


def _in_proj_kernel(x_ref, g_ref, sc_ref, sh_ref, w_ref, *o_refs):
    h = _norm_mod(x_ref[...], g_ref[...], sc_ref[0], sh_ref[0])
    p = _dot(h.astype(BF16), w_ref[...])
    off = 0
    for o_ref in o_refs:
        wdt = o_ref.shape[1]
        o_ref[...] = p[:, off:off + wdt]
        off += wdt


def _in_proj(x2, g, sc, sh, w_bf16, splits, seq, tm=512):
    t, d = x2.shape
    n = w_bf16.shape[1]
    tpb = seq // tm
    row = lambda i: (i, 0)
    per_b = lambda i: (i // tpb, 0, 0)
    return pl.pallas_call(
        _in_proj_kernel,
        grid=(t // tm,),
        in_specs=[pl.BlockSpec((tm, d), row),
                  pl.BlockSpec((1, d), lambda i: (0, 0)),
                  pl.BlockSpec((1, 1, d), per_b),
                  pl.BlockSpec((1, 1, d), per_b),
                  pl.BlockSpec((d, n), lambda i: (0, 0))],
        out_specs=[pl.BlockSpec((tm, s), row) for s in splits],
        out_shape=[jax.ShapeDtypeStruct((t, s), F32) for s in splits],
        compiler_params=_params(("parallel",), VMEM_LIMIT),
        name="in_proj",
    )(x2, g.reshape(1, d), sc, sh, w_bf16)


def _rwkv_prep_kernel(pm_ref, pp_ref, pn_ref, mup_ref, mun_ref, w0_ref, w2_ref, a0_ref, a2_ref,
                      kk_ref, ka_ref, rk_ref, g2_ref,
                      r_o, v_o, kk_o, kd_o, ad_o, lw_o, bonus_o, g_o):
    i = pl.program_id(1)
    n = pl.num_programs(1)
    p = pm_ref[...]
    ts = p.shape[0]
    rows = lax.broadcasted_iota(I32, (ts, 1), 0)
    prev_row = jnp.where(i > 0, pp_ref[HALO - 1:HALO, :], 0.0)
    next_row = jnp.where(i < n - 1, pn_ref[0:1, :], 0.0)
    prev = jnp.where(rows == 0, prev_row, pltpu.roll(p, 1, 0))
    nxt = jnp.where(rows == ts - 1, next_row, pltpu.roll(p, ts - 1, 0))
    pa = p + mup_ref[...] * (prev - p) + mun_ref[...] * (nxt - p)

    w = RWKV_WIDTH
    r = pa[:, 0:w]
    k = pa[:, w:2 * w]
    v = pa[:, 2 * w:3 * w]
    w_lo = pa[:, 3 * w:3 * w + W_LORA]
    a_lo = pa[:, 3 * w + W_LORA:3 * w + W_LORA + A_LORA]
    g_lo = pa[:, 3 * w + W_LORA + A_LORA:]

    ones = _group_matrix(w, HEAD_DIM, 1.0)
    kk = k * kk_ref[...]
    ss = _dot_exact_rhs(kk * kk, ones)
    kk = kk / jnp.maximum(jnp.sqrt(ss), 1e-12)
    tw = jnp.tanh(w_lo)
    r_o[...] = r
    v_o[...] = v
    kk_o[...] = kk
    bonus = jnp.zeros_like(r)
    for d in range(2):
        z = w0_ref[d] + jnp.dot(tw, w2_ref[d], preferred_element_type=F32, precision=HIGHEST)
        nz = -z
        softplus = jnp.maximum(nz, 0.0) + jnp.log(1.0 + jnp.exp(-jnp.abs(nz)))
        w_log = -softplus - 0.5
        lw_o[d] = -jnp.exp(w_log)
        a = _sigmoid(a0_ref[d] + jnp.dot(a_lo, a2_ref[d], preferred_element_type=F32,
                                         precision=HIGHEST))
        k_d = k * (1.0 + (a - 1.0) * ka_ref[...])
        ad_o[d] = a
        kd_o[d] = k_d
        bonus = bonus + _dot_exact_rhs(r * k_d * rk_ref[...], ones) * v
    bonus_o[...] = bonus
    g_o[...] = jnp.dot(_sigmoid(g_lo), g2_ref[...], preferred_element_type=F32, precision=HIGHEST)


def _rwkv_prep(pa, bsz, seq, mu_prev, mu_next, w0, w2, a0, a2, k_k, k_a, r_k, g2, ts=256):
    t = pa.shape[0]
    w = RWKV_WIDTH
    nt = seq // ts
    hb = ts // HALO
    main = lambda b, i: (b * nt + i, 0)
    prev = lambda b, i: (jnp.maximum((b * nt + i) * hb - 1, 0), 0)
    nxt = lambda b, i: (jnp.minimum((b * nt + i + 1) * hb, t // HALO - 1), 0)
    c0 = lambda b, i: (0, 0)
    c3 = lambda b, i: (0, 0, 0)
    dmain = lambda b, i: (0, b * nt + i, 0)
    tok = jax.ShapeDtypeStruct((t, w), F32)
    tok2 = jax.ShapeDtypeStruct((2, t, w), F32)
    return pl.pallas_call(
        _rwkv_prep_kernel,
        grid=(bsz, nt),
        in_specs=[pl.BlockSpec((ts, A_PROJ), main),
                  pl.BlockSpec((HALO, A_PROJ), prev),
                  pl.BlockSpec((HALO, A_PROJ), nxt),
                  pl.BlockSpec((1, A_PROJ), c0),
                  pl.BlockSpec((1, A_PROJ), c0),
                  pl.BlockSpec((2, 1, w), c3),
                  pl.BlockSpec((2, W_LORA, w), c3),
                  pl.BlockSpec((2, 1, w), c3),
                  pl.BlockSpec((2, A_LORA, w), c3),
                  pl.BlockSpec((1, w), c0),
                  pl.BlockSpec((1, w), c0),
                  pl.BlockSpec((1, w), c0),
                  pl.BlockSpec((G_LORA, w), c0)],
        out_specs=[pl.BlockSpec((ts, w), main)] * 3
                  + [pl.BlockSpec((2, ts, w), dmain)] * 3
                  + [pl.BlockSpec((ts, w), main)] * 2,
        out_shape=[tok, tok, tok, tok2, tok2, tok2, tok, tok],
        compiler_params=_params(("parallel", "parallel"), VMEM_LIMIT),
        name="rwkv_prep",
    )(pa, pa, pa, mu_prev.reshape(1, -1), mu_next.reshape(1, -1), w0.reshape(2, 1, w), w2,
      a0.reshape(2, 1, w), a2, k_k.reshape(1, w), k_a.reshape(1, w), r_k.reshape(1, w), g2)


def _wkv_scaled(r, kd, v, kk, a, lw, reverse):
    c = r.shape[0]
    ri = lax.broadcasted_iota(I32, (c, c), 0)
    ci = lax.broadcasted_iota(I32, (c, c), 1)
    if reverse:
        incl, strict, last = ci >= ri, ci > ri, 0
    else:
        incl, strict, last = ci <= ri, ci < ri, c - 1
    cum = _dot_exact_lhs(incl.astype(BF16), lw)
    cum_last = cum[last:last + 1, :]
    w_end = jnp.exp(cum_last - cum)
    w_inv = jnp.exp(-cum)
    b = kk * a
    return dict(rt=r * jnp.exp(cum), kt=kd * w_inv, bt=b * w_inv, at=-kk * jnp.exp(cum - lw),
                bk_t=jnp.concatenate([b * w_end, kd * w_end], axis=0).T,
                w_c=jnp.exp(cum_last), v=v, incl=incl, strict=strict)


def _wkv_chunks(ops, states):
    c = ops[0]["rt"].shape[0]
    hd = HEAD_DIM
    n_heads = ops[0]["rt"].shape[1] // hd
    streams = [(d, h) for d in range(len(ops)) for h in range(n_heads)]
    ri = lax.broadcasted_iota(I32, (c, c), 0)
    ci = lax.broadcasted_iota(I32, (c, c), 1)
    eye = (ri == ci).astype(F32)
    zeros = jnp.zeros((c, hd), F32)

    def head(d, h, name):
        return ops[d][name][:, h * hd:(h + 1) * hd]

    ll = [_dot_nt(jnp.concatenate([head(d, h, "at"), head(d, h, "rt")], axis=0),
                  jnp.concatenate([head(d, h, "bt"), head(d, h, "kt")], axis=0))
          for d, h in streams]
    l_ab = [jnp.where(ops[d]["strict"], m[:c, :c], 0.0) for (d, h), m in zip(streams, ll)]
    l_ak = [jnp.where(ops[d]["strict"], m[:c, c:], 0.0) for (d, h), m in zip(streams, ll)]
    l_rr = [jnp.concatenate([jnp.where(ops[d]["incl"], m[c:, :c], 0.0),
                             jnp.where(ops[d]["incl"], m[c:, c:], 0.0)], axis=1)
            for (d, h), m in zip(streams, ll)]
    tinv = [eye + m for m in l_ab]
    lp = l_ab
    for _ in range(max(1, (c - 1).bit_length() - 1)):
        lp = [_dot(m, m) for m in lp]
        tinv = [t + _dot(m, t) for m, t in zip(lp, tinv)]
    lakv = [_dot(m, head(d, h, "v")) for (d, h), m in zip(streams, l_ak)]
    x = [_dot(t, jnp.concatenate([head(d, h, "at"), u], axis=1))
         for (d, h), t, u in zip(streams, tinv, lakv)]
    o2 = [_dot(jnp.concatenate([lr, ops[d]["bk_t"][h * hd:(h + 1) * hd, :]], axis=0),
               jnp.concatenate([xx, jnp.concatenate([zeros, head(d, h, "v")], axis=1)], axis=0))
          for (d, h), lr, xx in zip(streams, l_rr, x)]
    o3 = [_dot(jnp.concatenate([head(d, h, "rt") + m[:c, :hd],
                                eye[:hd, :hd] * head(d, h, "w_c") + m[c:, :hd]], axis=0),
               states[d][:, h * hd:(h + 1) * hd])
          for (d, h), m in zip(streams, o2)]
    ys, sts = [], []
    for d in range(len(ops)):
        sel = [(m2, m3) for (dd, h), m2, m3 in zip(streams, o2, o3) if dd == d]
        ys.append(jnp.concatenate([m3[:c] + m2[:c, hd:] for m2, m3 in sel], axis=1))
        sts.append(jnp.concatenate([m3[c:] + m2[c:, hd:] for m2, m3 in sel], axis=1))
    return ys, sts


def _rwkv_scan_kernel(rf, vf, kkf, kdf, adf, lwf, rb, vb, kkb, kdb, adb, lwb, yf_o, yb_o, st_ref):
    @pl.when(pl.program_id(1) == 0)
    def _():
        st_ref[...] = jnp.zeros_like(st_ref)

    ops = [_wkv_scaled(rf[...], kdf[0], vf[...], kkf[...], adf[0], lwf[0], False),
           _wkv_scaled(rb[...], kdb[0], vb[...], kkb[...], adb[0], lwb[0], True)]
    ys, sts = _wkv_chunks(ops, [st_ref[0], st_ref[1]])
    yf_o[...] = ys[0]
    yb_o[...] = ys[1]
    st_ref[0] = sts[0]
    st_ref[1] = sts[1]


def _rwkv_scan(r, v, kk, kd, ad, lw, bsz, seq):
    t, w = r.shape
    nc = seq // CHUNK
    fwd = lambda b, i: (b * nc + i, 0)
    bwd = lambda b, i: (b * nc + nc - 1 - i, 0)
    fwd0 = lambda b, i: (0, b * nc + i, 0)
    bwd1 = lambda b, i: (1, b * nc + nc - 1 - i, 0)
    tok = pl.BlockSpec((CHUNK, w), fwd)
    tokb = pl.BlockSpec((CHUNK, w), bwd)
    dir0 = pl.BlockSpec((1, CHUNK, w), fwd0)
    dir1 = pl.BlockSpec((1, CHUNK, w), bwd1)
    return pl.pallas_call(
        _rwkv_scan_kernel,
        grid=(bsz, nc),
        in_specs=[tok, tok, tok, dir0, dir0, dir0, tokb, tokb, tokb, dir1, dir1, dir1],
        out_specs=[tok, tokb],
        out_shape=[jax.ShapeDtypeStruct((t, w), F32)] * 2,
        scratch_shapes=[pltpu.VMEM((2, HEAD_DIM, w), F32)],
        compiler_params=_params(("parallel", "arbitrary"), VMEM_LIMIT),
        name="rwkv_scan",
    )(r, v, kk, kd, ad, lw, r, v, kk, kd, ad, lw)


def _conv_kernel(pm_ref, pp_ref, pn_ref, cw_ref, cb_ref, g_ref, b_ref, o_ref, hbuf):
    i = pl.program_id(1)
    n = pl.num_programs(1)
    ts = pm_ref.shape[0]
    cwid = CONV_WIDTH

    def glu(p):
        return p[:, :cwid] * _sigmoid(p[:, cwid:])

    hbuf[0:HALO, :] = jnp.where(i > 0, glu(pp_ref[...]), 0.0)
    hbuf[HALO:HALO + ts, :] = glu(pm_ref[...])
    hbuf[HALO + ts:2 * HALO + ts, :] = jnp.where(i < n - 1, glu(pn_ref[...]), 0.0)
    acc = jnp.zeros((ts, cwid), F32)
    for k in range(CONV_KERNEL):
        off = HALO - CONV_PAD + k
        acc = acc + cw_ref[k:k + 1, :] * hbuf[off:off + ts, :]
    acc = acc + cb_ref[...]
    avg = _group_matrix(cwid, cwid // CONV_GROUPS, 1.0 / (cwid // CONV_GROUPS))
    mu = _dot_exact_rhs(acc, avg)
    xc = acc - mu
    var = _dot_exact_rhs(xc * xc, avg)
    y = xc * lax.rsqrt(var + GN_EPS) * g_ref[...] + b_ref[...]
    o_ref[...] = y * _sigmoid(y)


def _conformer_conv(pb, bsz, seq, conv_w, conv_b, gn_g, gn_b, ts=512):
    t = pb.shape[0]
    cwid = CONV_WIDTH
    nt = seq // ts
    hb = ts // HALO
    main = lambda b, i: (b * nt + i, 0)
    prev = lambda b, i: (jnp.maximum((b * nt + i) * hb - 1, 0), 0)
    nxt = lambda b, i: (jnp.minimum((b * nt + i + 1) * hb, t // HALO - 1), 0)
    c0 = lambda b, i: (0, 0)
    return pl.pallas_call(
        _conv_kernel,
        grid=(bsz, nt),
        in_specs=[pl.BlockSpec((ts, 2 * cwid), main),
                  pl.BlockSpec((HALO, 2 * cwid), prev),
                  pl.BlockSpec((HALO, 2 * cwid), nxt),
                  pl.BlockSpec((CONV_KERNEL, cwid), c0),
                  pl.BlockSpec((1, cwid), c0),
                  pl.BlockSpec((1, cwid), c0),
                  pl.BlockSpec((1, cwid), c0)],
        out_specs=pl.BlockSpec((ts, cwid), main),
        out_shape=jax.ShapeDtypeStruct((t, cwid), F32),
        scratch_shapes=[pltpu.VMEM((ts + 2 * HALO, cwid), F32)],
        compiler_params=_params(("parallel", "parallel"), VMEM_LIMIT),
        name="conformer_conv",
    )(pb, pb, pb, conv_w, conv_b.reshape(1, cwid), gn_g.reshape(1, cwid), gn_b.reshape(1, cwid))


def _mix_out_kernel(yf_ref, yb_ref, bonus_ref, g_ref, cv_ref, x_ref, lg_ref, lb_ref, w_ref,
                    ng_ref, gt_ref, o_ref):
    w = RWKV_WIDTH
    wkv = yf_ref[...] + yb_ref[...]
    avg = _group_matrix(w, HEAD_DIM, 1.0 / HEAD_DIM)
    mu = _dot_exact_rhs(wkv, avg)
    xc = wkv - mu
    var = _dot_exact_rhs(xc * xc, avg)
    y = xc * lax.rsqrt(var + LNX_EPS) * lg_ref[...] + lb_ref[...] + bonus_ref[...]
    ya = y * g_ref[...]
    cat = jnp.concatenate([ya.astype(BF16), cv_ref[...].astype(BF16)], axis=1)
    z = _dot(cat, w_ref[...])
    o_ref[...] = x_ref[...] + gt_ref[0] * _rms_norm(z, ng_ref[...])


def _mix_out(yf, yb, bonus, g, cv, x2, lnx_g, lnx_b, w_out_bf16, norm_g, gt, seq, tm=512):
    t, d = x2.shape
    w = RWKV_WIDTH
    tpb = seq // tm
    row = lambda i: (i, 0)
    c0 = lambda i: (0, 0)
    half = pl.BlockSpec((tm, w), row)
    return pl.pallas_call(
        _mix_out_kernel,
        grid=(t // tm,),
        in_specs=[half, half, half, half, half,
                  pl.BlockSpec((tm, d), row),
                  pl.BlockSpec((1, w), c0),
                  pl.BlockSpec((1, w), c0),
                  pl.BlockSpec((d, d), c0),
                  pl.BlockSpec((1, d), c0),
                  pl.BlockSpec((1, 1, d), lambda i: (i // tpb, 0, 0))],
        out_specs=pl.BlockSpec((tm, d), row),
        out_shape=jax.ShapeDtypeStruct((t, d), F32),
        compiler_params=_params(("parallel",), VMEM_LIMIT),
        name="mix_out",
    )(yf, yb, bonus, g, cv, x2, lnx_g.reshape(1, w), lnx_b.reshape(1, w), w_out_bf16,
      norm_g.reshape(1, d), gt)


def _attn_kernel(sink_ref, q_ref, kp_ref, kc_ref, kn_ref, vp_ref, vc_ref, vn_ref, bias_ref, o_ref):
    i = pl.program_id(1)
    n = pl.num_programs(1)
    blk = ATTN_BLOCK
    hd = HEAD_DIM
    k = jnp.concatenate([kp_ref[...], kc_ref[...], kn_ref[...]], axis=0).astype(BF16)
    v = jnp.concatenate([vp_ref[...], vc_ref[...], vn_ref[...]], axis=0).astype(BF16)
    col = lax.broadcasted_iota(I32, (1, 3 * blk), 1)
    valid = ((col >= blk) | (i > 0)) & ((col < 2 * blk) | (i < n - 1))
    scale = hd ** -0.5
    kvs = range(ATTN_KV_HEADS)
    qs = [jnp.concatenate([q_ref[:, (kvh * ATTN_GROUP + g) * hd:(kvh * ATTN_GROUP + g + 1) * hd]
                           for g in range(ATTN_GROUP)], axis=0).astype(BF16) for kvh in kvs]
    scs = [_dot_nt(qs[kvh], k[:, kvh * hd:(kvh + 1) * hd]) for kvh in kvs]
    ps, denoms = [], []
    for kvh in kvs:
        sc = jnp.where(valid, scs[kvh] * scale + bias_ref[kvh], NEG)
        sink = sink_ref[kvh]
        m = jnp.maximum(jnp.max(sc, axis=-1, keepdims=True), sink)
        p = jnp.exp(sc - m)
        denoms.append(jnp.sum(p, axis=-1, keepdims=True) + jnp.exp(sink - m))
        ps.append(p.astype(BF16))
    for kvh in kvs:
        o = _dot(ps[kvh], v[:, kvh * hd:(kvh + 1) * hd]) / denoms[kvh]
        for g in range(ATTN_GROUP):
            h = kvh * ATTN_GROUP + g
            o_ref[:, h * hd:(h + 1) * hd] = o[g * blk:(g + 1) * blk, :]


def _attention(qkv, bias, sink, bsz, seq):
    t = qkv.shape[0]
    rows = ATTN_GROUP * ATTN_BLOCK
    bias = bias.reshape(ATTN_KV_HEADS, rows, 3 * ATTN_BLOCK)
    sink = jnp.repeat(sink.astype(F32), ATTN_BLOCK).reshape(ATTN_KV_HEADS, rows, 1)
    blk = ATTN_BLOCK
    nb = seq // blk
    kcol = Q_WIDTH // KV_WIDTH
    vcol = kcol + 1
    cur = lambda b, i: b * nb + i
    prv = lambda b, i: b * nb + jnp.maximum(i - 1, 0)
    nxt = lambda b, i: b * nb + jnp.minimum(i + 1, nb - 1)
    kv = lambda rowf, c: pl.BlockSpec((blk, KV_WIDTH), lambda b, i: (rowf(b, i), c))
    return pl.pallas_call(
        _attn_kernel,
        grid=(bsz, nb),
        in_specs=[pl.BlockSpec((ATTN_KV_HEADS, rows, 1), lambda b, i: (0, 0, 0)),
                  pl.BlockSpec((blk, Q_WIDTH), lambda b, i: (cur(b, i), 0)),
                  kv(prv, kcol), kv(cur, kcol), kv(nxt, kcol),
                  kv(prv, vcol), kv(cur, vcol), kv(nxt, vcol),
                  pl.BlockSpec((ATTN_KV_HEADS, rows, 3 * blk), lambda b, i: (0, 0, 0))],
        out_specs=pl.BlockSpec((blk, Q_WIDTH), lambda b, i: (cur(b, i), 0)),
        out_shape=jax.ShapeDtypeStruct((t, Q_WIDTH), F32),
        compiler_params=_params(("parallel", "parallel"), VMEM_LIMIT),
        name="attention",
    )(sink, qkv, qkv, qkv, qkv, qkv, qkv, qkv, bias)


def _t5_bucket(rel):
    n = -rel
    half = N_BUCKETS // 2
    ret = jnp.where(n < 0, half, 0)
    n = jnp.abs(n)
    max_exact = half // 2
    nf = jnp.maximum(n, 1).astype(F32)
    large = max_exact + (jnp.log(nf / max_exact) / jnp.log(MAX_DISTANCE / max_exact)
                         * (half - max_exact)).astype(I32)
    large = jnp.minimum(large, half - 1)
    return ret + jnp.where(n < max_exact, n, large)


def _attn_bias(rel_bias):
    blk = ATTN_BLOCK
    rel = jnp.arange(3 * blk)[None, :] - blk - jnp.arange(blk)[:, None]
    bias = jnp.transpose(rel_bias[_t5_bucket(rel)].astype(F32), (2, 0, 1))
    return jnp.where(jnp.abs(rel) <= WINDOW, bias, NEG)


def _out_proj_kernel(y_ref, x_ref, w_ref, ng_ref, gt_ref, o_ref):
    z = _dot(y_ref[...].astype(BF16), w_ref[...])
    o_ref[...] = x_ref[...] + gt_ref[0] * _rms_norm(z, ng_ref[...])


def _out_proj(y, x2, w_bf16, norm_g, gt, seq, tm=512):
    t, d = x2.shape
    tpb = seq // tm
    row = lambda i: (i, 0)
    c0 = lambda i: (0, 0)
    return pl.pallas_call(
        _out_proj_kernel,
        grid=(t // tm,),
        in_specs=[pl.BlockSpec((tm, d), row),
                  pl.BlockSpec((tm, d), row),
                  pl.BlockSpec((d, d), c0),
                  pl.BlockSpec((1, d), c0),
                  pl.BlockSpec((1, 1, d), lambda i: (i // tpb, 0, 0))],
        out_specs=pl.BlockSpec((tm, d), row),
        out_shape=jax.ShapeDtypeStruct((t, d), F32),
        compiler_params=_params(("parallel",), VMEM_LIMIT),
        name="out_proj",
    )(y, x2, w_bf16, norm_g.reshape(1, d), gt)


def _router_kernel(x_ref, g_ref, sc_ref, sh_ref, rw_ref, rb_ref, idx_o, gate_o, rank_o, cnt_o,
                   cnt_ref):
    @pl.when(pl.program_id(0) == 0)
    def _():
        cnt_ref[...] = jnp.zeros_like(cnt_ref)

    h = _norm_mod(x_ref[...], g_ref[...], sc_ref[0], sh_ref[0])
    tm = h.shape[0]
    ne = N_EXPERTS
    logits = lax.dot_general(rw_ref[...], h, (((1,), (1,)), ((), ())),
                             preferred_element_type=F32, precision=HIGHEST) + rb_ref[...]
    eidx = lax.broadcasted_iota(I32, (ne, tm), 0)
    vals, sels = [], []
    for k in range(TOP_K):
        m = jnp.max(logits, axis=0, keepdims=True)
        idx = jnp.min(jnp.where(logits == m, eidx, ne), axis=0, keepdims=True)
        sel = eidx == idx
        vals.append(m)
        sels.append(sel)
        idx_o[k:k + 1, :] = idx
        logits = jnp.where(sel, -jnp.inf, logits)
    es = [jnp.exp(vk - vals[0]) for vk in vals]
    tot = es[0] + es[1] + es[2] + es[3]
    for k in range(TOP_K):
        gate_o[k:k + 1, :] = es[k] / tot
    onehot = (sels[0] | sels[1] | sels[2] | sels[3])
    before = (lax.broadcasted_iota(I32, (tm, tm), 0) < lax.broadcasted_iota(I32, (tm, tm), 1))
    cum = _dot(onehot.astype(BF16), before.astype(BF16)) + cnt_ref[:, 0:1]
    for k in range(TOP_K):
        rank_o[k:k + 1, :] = jnp.sum(jnp.where(sels[k], cum, 0.0), axis=0,
                                     keepdims=True).astype(I32)
    cnt_ref[...] = cnt_ref[...] + jnp.sum(onehot.astype(F32), axis=1, keepdims=True)
    cnt_o[...] = cnt_ref[...]


def _router(x2, g, sc, sh, router_w_t, router_b, seq, tm=512):
    t, d = x2.shape
    ne = N_EXPERTS
    tpb = seq // tm
    per_b = lambda i: (i // tpb, 0, 0)
    c0 = lambda i: (0, 0)
    col = lambda i: (0, i)
    return pl.pallas_call(
        _router_kernel,
        grid=(t // tm,),
        in_specs=[pl.BlockSpec((tm, d), lambda i: (i, 0)),
                  pl.BlockSpec((1, d), c0),
                  pl.BlockSpec((1, 1, d), per_b),
                  pl.BlockSpec((1, 1, d), per_b),
                  pl.BlockSpec((ne, d), c0),
                  pl.BlockSpec((ne, 1), c0)],
        out_specs=[pl.BlockSpec((TOP_K, tm), col),
                   pl.BlockSpec((TOP_K, tm), col),
                   pl.BlockSpec((TOP_K, tm), col),
                   pl.BlockSpec((ne, 128), c0)],
        out_shape=[jax.ShapeDtypeStruct((TOP_K, t), I32),
                   jax.ShapeDtypeStruct((TOP_K, t), F32),
                   jax.ShapeDtypeStruct((TOP_K, t), I32),
                   jax.ShapeDtypeStruct((ne, 128), F32)],
        scratch_shapes=[pltpu.VMEM((ne, 128), F32)],
        compiler_params=_params(("arbitrary",), VMEM_LIMIT),
        name="router",
    )(x2, g.reshape(1, d), sc, sh, router_w_t, router_b.reshape(ne, 1))


def _pack_rows(h):
    half = h.shape[1] // 2
    hi = lax.bitcast_convert_type(h[:, :half].astype(BF16).astype(F32), U32)
    lo = lax.bitcast_convert_type(h[:, half:].astype(BF16).astype(F32), U32)
    return (hi & jnp.uint32(0xFFFF0000)) | (lo >> 16)


def _unpack_rows(p):
    hi = lax.bitcast_convert_type(p & jnp.uint32(0xFFFF0000), F32)
    lo = lax.bitcast_convert_type(p << 16, F32)
    return jnp.concatenate([hi, lo], axis=1)


def _tile_indices(dest_hbm, dest_smem, sem):
    i = pl.program_id(0)
    n = pl.num_programs(0)
    slot = i % 2

    def fetch(step, s):
        return pltpu.make_async_copy(dest_hbm.at[step], dest_smem.at[s], sem)

    @pl.when(i == 0)
    def _():
        fetch(0, 0).start()

    fetch(i, slot).wait()

    @pl.when(i + 1 < n)
    def _():
        fetch(i + 1, 1 - slot).start()

    return slot


def _dispatch_kernel(x_ref, g_ref, sc_ref, sh_ref, dest_hbm, xs_in, xs_hbm, hbuf, dest_smem, sems):
    del xs_in
    tm = x_ref.shape[0]
    hbuf[...] = _pack_rows(_norm_mod(x_ref[...], g_ref[...], sc_ref[0], sh_ref[0]))
    slot = _tile_indices(dest_hbm, dest_smem, sems.at[0])

    def issue(grp, carry):
        for u in range(ROW_UNROLL):
            row = grp * ROW_UNROLL + u
            for k in range(TOP_K):
                pltpu.make_async_copy(hbuf.at[pl.ds(row, 1), :],
                                      xs_hbm.at[pl.ds(dest_smem[slot, k * tm + row], 1), :],
                                      sems.at[1]).start()
        return carry

    lax.fori_loop(0, tm // ROW_UNROLL, issue, 0)

    for k in range(TOP_K):
        pltpu.make_async_copy(hbuf, xs_hbm.at[pl.ds(0, tm), :], sems.at[1]).wait()


def _dispatch(x2, g, sc, sh, dest_tiles, xs_init, seq, tm):
    t, d = x2.shape
    tpb = seq // tm
    per_b = lambda i: (i // tpb, 0, 0)
    return pl.pallas_call(
        _dispatch_kernel,
        grid=(t // tm,),
        in_specs=[pl.BlockSpec((tm, d), lambda i: (i, 0)),
                  pl.BlockSpec((1, d), lambda i: (0, 0)),
                  pl.BlockSpec((1, 1, d), per_b),
                  pl.BlockSpec((1, 1, d), per_b),
                  pl.BlockSpec(memory_space=pl.ANY),
                  pl.BlockSpec(memory_space=pl.ANY)],
        out_specs=pl.BlockSpec(memory_space=pl.ANY),
        out_shape=jax.ShapeDtypeStruct(xs_init.shape, U32),
        scratch_shapes=[pltpu.VMEM((tm, d // 2), U32),
                        pltpu.SMEM((2, TOP_K * tm), I32),
                        pltpu.SemaphoreType.DMA((2,))],
        input_output_aliases={5: 0},
        compiler_params=_params(("arbitrary",), VMEM_LIMIT),
        name="moe_dispatch",
    )(x2, g.reshape(1, d), sc, sh, dest_tiles, xs_init)


def _expert_kernel(be_ref, first_ref, used_ref, xs_ref, w1_ref, b1_ref, w2_ref, b2_ref, ys_ref,
                   w1p, w2p):
    i = pl.program_id(0)
    f2 = w1_ref.shape[3]
    pw = 256

    @pl.when((first_ref[i] == 1) & (i < used_ref[0]))
    def _():
        r = lax.broadcasted_iota(I32, (pw, pw), 0)
        c = lax.broadcasted_iota(I32, (pw, pw), 1)
        src = jnp.where(c < pw // 2, 2 * c, 2 * (c - pw // 2) + 1)
        perm = (r == src).astype(BF16)
        for j in range(f2 // pw):
            w1p[:, j * pw:(j + 1) * pw] = _dot(w1_ref[0, 0, :, j * pw:(j + 1) * pw].astype(BF16),
                                               perm).astype(BF16)
        w2p[...] = w2_ref[0, 0].astype(BF16)

    @pl.when(i < used_ref[0])
    def _():
        x = _unpack_rows(xs_ref[...]).astype(BF16)
        hcat = _dot(x, w1p[...]) + b1_ref[0, 0]
        acts = []
        for j in range(f2 // pw):
            x_glu = jnp.minimum(hcat[:, j * pw:j * pw + pw // 2], SWIGLU_LIMIT)
            x_lin = jnp.clip(hcat[:, j * pw + pw // 2:(j + 1) * pw], -SWIGLU_LIMIT, SWIGLU_LIMIT)
            acts.append(x_glu * _sigmoid(SWIGLU_ALPHA * x_glu) * (x_lin + 1.0))
        act = jnp.concatenate(acts, axis=1).astype(BF16)
        ys_ref[...] = _pack_rows(_dot(act, w2p[...]) + b2_ref[0, 0])

    @pl.when(i >= used_ref[0])
    def _():
        ys_ref[...] = jnp.zeros_like(ys_ref)


def _experts(xs, blk_expert, blk_first, n_used, layer, w1, b1_perm, w2, b2):
    n_slots, dp = xs.shape
    _, ne, d, f2 = w1.shape
    f = w2.shape[2]
    bm = MOE_BLOCK
    wmap = lambda i, be, fi, us: (layer, be[i], 0, 0)
    grid_spec = pltpu.PrefetchScalarGridSpec(
        num_scalar_prefetch=3,
        grid=(n_slots // bm,),
        in_specs=[pl.BlockSpec((bm, dp), lambda i, be, fi, us: (i, 0)),
                  pl.BlockSpec((1, 1, d, f2), wmap),
                  pl.BlockSpec((1, 1, 1, f2), wmap),
                  pl.BlockSpec((1, 1, f, d), wmap),
                  pl.BlockSpec((1, 1, 1, d), wmap)],
        out_specs=pl.BlockSpec((bm, dp), lambda i, be, fi, us: (i, 0)),
        scratch_shapes=[pltpu.VMEM((d, f2), BF16), pltpu.VMEM((f, d), BF16)],
    )
    depth = w1.shape[0]
    return pl.pallas_call(
        _expert_kernel,
        grid_spec=grid_spec,
        out_shape=jax.ShapeDtypeStruct((n_slots, dp), U32),
        compiler_params=_params(("arbitrary",), VMEM_LIMIT),
        name="moe_experts",
    )(blk_expert, blk_first, n_used, xs, w1, b1_perm.reshape(depth, ne, 1, f2), w2,
      b2.reshape(depth, ne, 1, d))


def _combine_kernel(x_ref, gate_ref, ng_ref, gt_ref, dest_hbm, ys_hbm, o_ref, ybuf, dest_smem, sems):
    tm = x_ref.shape[0]
    slot = _tile_indices(dest_hbm, dest_smem, sems.at[0])

    def issue(grp, carry):
        for u in range(TOP_K * ROW_UNROLL):
            j = grp * (TOP_K * ROW_UNROLL) + u
            pltpu.make_async_copy(ys_hbm.at[pl.ds(dest_smem[slot, j], 1), :],
                                  ybuf.at[pl.ds(j, 1), :], sems.at[1]).start()
        return carry

    lax.fori_loop(0, tm // ROW_UNROLL, issue, 0)
    pltpu.make_async_copy(ys_hbm.at[pl.ds(0, TOP_K * tm), :], ybuf, sems.at[1]).wait()
    y = jnp.zeros(x_ref.shape, F32)
    for k in range(TOP_K):
        y = y + gate_ref[:, k:k + 1] * _unpack_rows(ybuf[k * tm:(k + 1) * tm, :])
    o_ref[...] = x_ref[...] + gt_ref[0] * _rms_norm(y, ng_ref[...])


def _combine(x2, gates_tk, norm_g, gt, dest_tiles, ys, seq, tm):
    t, d = x2.shape
    tpb = seq // tm
    return pl.pallas_call(
        _combine_kernel,
        grid=(t // tm,),
        in_specs=[pl.BlockSpec((tm, d), lambda i: (i, 0)),
                  pl.BlockSpec((tm, TOP_K), lambda i: (i, 0)),
                  pl.BlockSpec((1, d), lambda i: (0, 0)),
                  pl.BlockSpec((1, 1, d), lambda i: (i // tpb, 0, 0)),
                  pl.BlockSpec(memory_space=pl.ANY),
                  pl.BlockSpec(memory_space=pl.ANY)],
        out_specs=pl.BlockSpec((tm, d), lambda i: (i, 0)),
        out_shape=jax.ShapeDtypeStruct((t, d), F32),
        scratch_shapes=[pltpu.VMEM((TOP_K * tm, d // 2), U32),
                        pltpu.SMEM((2, TOP_K * tm), I32),
                        pltpu.SemaphoreType.DMA((2,))],
        compiler_params=_params(("arbitrary",), VMEM_LIMIT),
        name="moe_combine",
    )(x2, gates_tk, norm_g.reshape(1, d), gt, dest_tiles, ys)


def _moe(x2, norm_pre, sc, sh, norm_post, gt, router_w, router_b, layer, w1, b1_perm, w2, b2, seq,
         tm=512):
    t, d = x2.shape
    ne = N_EXPERTS
    bm = MOE_BLOCK
    top_idx, gates, rank, cnt = _router(x2, norm_pre, sc, sh, router_w.T, router_b, seq)
    counts = cnt[:, 0].astype(I32)
    padded = ((counts + bm - 1) // bm) * bm
    padded_end = jnp.cumsum(padded)
    padded_start = padded_end - padded
    start_of = jnp.sum(jnp.where(top_idx[..., None] == jnp.arange(ne), padded_start, 0), axis=-1)
    dest = start_of + rank
    n_slots = t * TOP_K + ne * bm
    n_blocks = n_slots // bm
    blk_row = jnp.arange(n_blocks, dtype=I32) * bm
    blk_expert = jnp.minimum(jnp.sum((padded_end[None, :] <= blk_row[:, None]).astype(I32), axis=1),
                             ne - 1)
    blk_first = jnp.concatenate([jnp.ones((1,), I32),
                                 (blk_expert[1:] != blk_expert[:-1]).astype(I32)])
    n_used = (padded_end[-1:] // bm).astype(I32)
    dest_tiles = dest.reshape(TOP_K, t // tm, tm).transpose(1, 0, 2).reshape(t // tm, TOP_K * tm)
    xs = _dispatch(x2, norm_pre, sc, sh, dest_tiles, jnp.zeros((n_slots, d // 2), U32), seq, tm)
    ys = _experts(xs, blk_expert, blk_first, n_used, layer, w1, b1_perm, w2, b2)
    return _combine(x2, gates.T, norm_post, gt, dest_tiles, ys, seq, tm)


def kernel(x, c, mod_w, mod_b, norm_mix_pre, norm_mix_post, norm_ffn_pre, norm_ffn_post, ab_w_in, ab_mu_prev, ab_mu_next, rwkv_w0, rwkv_w2, rwkv_a0, rwkv_a2, rwkv_k_k, rwkv_k_a, rwkv_r_k, rwkv_lnx_g, rwkv_lnx_b, rwkv_g2, conv_w, conv_b, conv_gn_g, conv_gn_b, ab_w_out, attn_w_qkv, attn_sink, attn_w_o, rel_bias, router_w, router_b, exp_w1, exp_b1, exp_w2, exp_b2):
    bsz, seq, d = x.shape
    depth = mod_w.shape[0]
    t = bsz * seq
    x2 = x.reshape(t, d)
    mod = _mod_all(c, mod_w, mod_b)
    bias = _attn_bias(rel_bias)
    ne, f2 = exp_b1.shape[1:]
    b1_perm = exp_b1.reshape(depth, ne, f2 // 256, 128, 2).swapaxes(-1, -2).reshape(depth, ne, f2)
    for layer in range(depth):
        j = layer // 2
        sh1, sc1, gt1, sh2, sc2, gt2 = [m.reshape(bsz, 1, d) for m in jnp.split(mod[layer], 6, axis=-1)]
        if layer % 2 == 0:
            pa, pb = _in_proj(x2, norm_mix_pre[layer], sc1, sh1, ab_w_in[j].astype(BF16),
                              (A_PROJ, 2 * CONV_WIDTH), seq)
            r, v, kk, kd, ad, lw, bonus, g = _rwkv_prep(
                pa, bsz, seq, ab_mu_prev[j], ab_mu_next[j], rwkv_w0[j], rwkv_w2[j], rwkv_a0[j],
                rwkv_a2[j], rwkv_k_k[j], rwkv_k_a[j], rwkv_r_k[j], rwkv_g2[j])
            yf, yb = _rwkv_scan(r, v, kk, kd, ad, lw, bsz, seq)
            cv = _conformer_conv(pb, bsz, seq, conv_w[j], conv_b[j], conv_gn_g[j], conv_gn_b[j])
            x2 = _mix_out(yf, yb, bonus, g, cv, x2, rwkv_lnx_g[j], rwkv_lnx_b[j],
                          ab_w_out[j].astype(BF16), norm_mix_post[layer], gt1, seq)
        else:
            (qkv,) = _in_proj(x2, norm_mix_pre[layer], sc1, sh1, attn_w_qkv[j].astype(BF16),
                              (Q_WIDTH + 2 * KV_WIDTH,), seq)
            y = _attention(qkv, bias, attn_sink[j], bsz, seq)
            x2 = _out_proj(y, x2, attn_w_o[j].astype(BF16), norm_mix_post[layer], gt1, seq)
        x2 = _moe(x2, norm_ffn_pre[layer], sc2, sh2, norm_ffn_post[layer], gt2, router_w[layer],
                  router_b[layer], layer, exp_w1, b1_perm, exp_w2, exp_b2, seq)
    return x2.reshape(bsz, seq, d)
```
